```python
import math
import jax, jax.numpy as jnp
from jax import lax
import numpy as np

D_MODEL = 2048
BATCH = 2
SEQ = 16384
DEPTH = 4

HEAD_DIM = 128
DIFF_HEADS = D_MODEL // (2 * HEAD_DIM)
DIFF_HALF = HEAD_DIM // 2
DIFF_WIDTH = DIFF_HEADS * HEAD_DIM
GQA_HEADS = D_MODEL // (2 * HEAD_DIM)
GQA_KV_HEADS = 2
GQA_GROUP = GQA_HEADS // GQA_KV_HEADS
GQA_WIDTH = GQA_HEADS * HEAD_DIM
GQA_KV_WIDTH = GQA_KV_HEADS * HEAD_DIM
MIX_WIDTH = DIFF_WIDTH + GQA_WIDTH
IN_PROJ_WIDTH = 3 * DIFF_WIDTH + GQA_WIDTH + 2 * GQA_KV_WIDTH
SPLITS = [DIFF_WIDTH, 2 * DIFF_WIDTH, 3 * DIFF_WIDTH,
          3 * DIFF_WIDTH + GQA_WIDTH, 3 * DIFF_WIDTH + GQA_WIDTH + GQA_KV_WIDTH]
Q_BLOCK = 128
GRID_W = 64
ROPE_THETA = 10000.0
ROPE_AXIS_DIM = HEAD_DIM // 2
D_FF_DENSE = ((8 * D_MODEL // 3 + 127) // 128) * 128
N_EXPERTS = 8
TOP_K = 2
D_FF_EXPERT = 7 * D_MODEL // 2
N_DENSE = (DEPTH + 1) // 2
N_MOE = DEPTH // 2
EPS = 1e-6

kernel_name = "hybrid_diffattn_axialgqa_moe_encoder"


def _rms(x, g):
    xf = x.astype(jnp.float32)
    y = xf * lax.rsqrt(jnp.mean(xf * xf, axis=-1, keepdims=True) + EPS)
    return (y * g.astype(jnp.float32)).astype(x.dtype)


def _alibi_slopes(n_heads):
    s = 2.0 ** (-8.0 * (np.arange(n_heads) + 1) / n_heads)
    return jnp.asarray(s, dtype=jnp.float32)


def _axial_rope_tables(S):
    rows_n = S // GRID_W
    row = jnp.repeat(jnp.arange(rows_n, dtype=jnp.float32), GRID_W, total_repeat_length=S)
    col = jnp.tile(jnp.arange(GRID_W, dtype=jnp.float32), rows_n)
    inv = ROPE_THETA ** (-jnp.arange(0, ROPE_AXIS_DIM, 2, dtype=jnp.float32) / ROPE_AXIS_DIM)
    ang = jnp.concatenate([row[:, None] * inv, col[:, None] * inv], axis=-1)
    return jnp.cos(ang), jnp.sin(ang)


def _apply_rope(x, cos, sin):
    xf = x.astype(jnp.float32).reshape(*x.shape[:-1], HEAD_DIM // 2, 2)
    x0, x1 = xf[..., 0], xf[..., 1]
    c = cos[None, :, None, :]
    s = sin[None, :, None, :]
    out = jnp.stack([x0 * c - x1 * s, x0 * s + x1 * c], axis=-1)
    return out.reshape(x.shape).astype(x.dtype)


def _diff_attention(q, k, v, lam, slopes):
    B, S = q.shape[0], q.shape[1]
    H, E = v.shape[2], v.shape[3]
    q = q * (DIFF_HALF ** -0.5)
    key_pos = jnp.arange(S)

    def block(start):
        qb = lax.dynamic_slice_in_dim(q, start, Q_BLOCK, axis=1)
        s = jnp.einsum('bqmhd,bkmhd->bmhqk', qb, k, preferred_element_type=jnp.float32)
        dist = jnp.abs((start + jnp.arange(Q_BLOCK))[:, None] - key_pos[None, :]).astype(jnp.float32)
        s = s - slopes[:, None, None] * dist
        p = jax.nn.softmax(s, axis=-1)
        a = p[:, 0] - lam * p[:, 1]
        return jnp.einsum('bhqk,bkhe->bqhe', a.astype(v.dtype), v)

    out = lax.map(block, jnp.arange(0, S, Q_BLOCK))
    return jnp.moveaxis(out, 0, 1).reshape(B, S, H, E)


def _gqa_attention(q, k, v):
    B, S = q.shape[0], q.shape[1]
    q = q * (HEAD_DIM ** -0.5)

    def block(start):
        qb = lax.dynamic_slice_in_dim(q, start, Q_BLOCK, axis=1)
        s = jnp.einsum('bqgrd,bkgd->bgrqk', qb, k, preferred_element_type=jnp.float32)
        p = jax.nn.softmax(s, axis=-1)
        return jnp.einsum('bgrqk,bkgd->bqgrd', p.astype(v.dtype), v)

    out = lax.map(block, jnp.arange(0, S, Q_BLOCK))
    return jnp.moveaxis(out, 0, 1).reshape(B, S, GQA_WIDTH)


def _swiglu(h, wg, wu, wd):
    return (jax.nn.silu(h @ wg) * (h @ wu)) @ wd


def _moe(h, router, wg, wu, wd):
    logits = jnp.dot(h, router, preferred_element_type=jnp.float32)
    top_val, top_idx = lax.top_k(logits, TOP_K)
    gates = jax.nn.softmax(top_val, axis=-1)
    comb = jnp.sum(jax.nn.one_hot(top_idx, N_EXPERTS, dtype=jnp.float32) * gates[..., None], axis=1)
    out = jnp.zeros_like(h)
    for e in range(N_EXPERTS):
        out = out + comb[:, e:e + 1].astype(h.dtype) * _swiglu(h, wg[e], wu[e], wd[e])
    return out


def setup_inputs(seed: int = 0) -> dict:
    key = jax.random.key(seed)
    ks = jax.random.split(key, 20)
    f32 = jnp.float32
    nrm = lambda k, shape, scale: jax.random.normal(k, shape, f32) * scale
    gain = lambda k, shape: 1.0 + 0.02 * jax.random.normal(k, shape, f32)
    return {
        "x": jax.random.normal(ks[0], (BATCH, SEQ, D_MODEL), f32),
        "attn_norm": gain(ks[1], (DEPTH, D_MODEL)),
        "w_in": nrm(ks[2], (DEPTH, D_MODEL, IN_PROJ_WIDTH), D_MODEL ** -0.5),
        "diff_q_norm": gain(ks[3], (DEPTH, DIFF_HALF)),
        "diff_k_norm": gain(ks[4], (DEPTH, DIFF_HALF)),
        "diff_lambda": nrm(ks[5], (DEPTH, 4, DIFF_HALF), 0.1),
        "diff_subln": gain(ks[6], (DEPTH, HEAD_DIM)),
        "gqa_q_norm": gain(ks[7], (DEPTH, HEAD_DIM)),
        "gqa_k_norm": gain(ks[8], (DEPTH, HEAD_DIM)),
        "w_out": nrm(ks[9], (DEPTH, MIX_WIDTH, D_MODEL), MIX_WIDTH ** -0.5),
        "ffn_norm": gain(ks[10], (DEPTH, D_MODEL)),
        "dense_w_gate": nrm(ks[11], (N_DENSE, D_MODEL, D_FF_DENSE), D_MODEL ** -0.5),
        "dense_w_up": nrm(ks[12], (N_DENSE, D_MODEL, D_FF_DENSE), D_MODEL ** -0.5),
        "dense_w_down": nrm(ks[13], (N_DENSE, D_FF_DENSE, D_MODEL), D_FF_DENSE ** -0.5),
        "moe_router": nrm(ks[14], (N_MOE, D_MODEL, N_EXPERTS), D_MODEL ** -0.5),
        "moe_w_gate": nrm(ks[15], (N_MOE, N_EXPERTS, D_MODEL, D_FF_EXPERT), D_MODEL ** -0.5),
        "moe_w_up": nrm(ks[16], (N_MOE, N_EXPERTS, D_MODEL, D_FF_EXPERT), D_MODEL ** -0.5),
        "moe_w_down": nrm(ks[17], (N_MOE, N_EXPERTS, D_FF_EXPERT, D_MODEL), D_FF_EXPERT ** -0.5),
    }


def reference(x, attn_norm, w_in, diff_q_norm, diff_k_norm, diff_lambda, diff_subln,
              gqa_q_norm, gqa_k_norm, w_out, ffn_norm, dense_w_gate, dense_w_up,
              dense_w_down, moe_router, moe_w_gate, moe_w_up, moe_w_down):
    B, S, D = x.shape
    cos, sin = _axial_rope_tables(S)
    slopes = _alibi_slopes(DIFF_HEADS)
    for l in range(DEPTH):
        n = _rms(x, attn_norm[l])
        proj = jnp.einsum('bsd,de->bse', n, w_in[l])
        aq, ak, av, bq, bk, bv = jnp.split(proj, SPLITS, axis=-1)

        aq = _rms(aq.reshape(B, S, DIFF_HEADS, 2, DIFF_HALF).swapaxes(2, 3), diff_q_norm[l])
        ak = _rms(ak.reshape(B, S, DIFF_HEADS, 2, DIFF_HALF).swapaxes(2, 3), diff_k_norm[l])
        av = av.reshape(B, S, DIFF_HEADS, HEAD_DIM)
        lam_init = 0.8 - 0.6 * math.exp(-0.3 * l)
        lp = diff_lambda[l].astype(jnp.float32)
        lam = jnp.exp(jnp.sum(lp[0] * lp[1])) - jnp.exp(jnp.sum(lp[2] * lp[3])) + lam_init
        ao = _diff_attention(aq, ak, av, lam, slopes)
        ao = (_rms(ao, diff_subln[l]) * (1.0 - lam_init)).reshape(B, S, DIFF_WIDTH)

        bq = _apply_rope(_rms(bq.reshape(B, S, GQA_HEADS, HEAD_DIM), gqa_q_norm[l]), cos, sin)
        bk = _apply_rope(_rms(bk.reshape(B, S, GQA_KV_HEADS, HEAD_DIM), gqa_k_norm[l]), cos, sin)
        bv = bv.reshape(B, S, GQA_KV_HEADS, HEAD_DIM)
        bo = _gqa_attention(bq.reshape(B, S, GQA_KV_HEADS, GQA_GROUP, HEAD_DIM), bk, bv)

        x = x + jnp.einsum('bse,ed->bsd', jnp.concatenate([ao, bo], axis=-1), w_out[l])

        h = _rms(x, ffn_norm[l])
        if l % 2 == 0:
            i = l // 2
            x = x + _swiglu(h, dense_w_gate[i], dense_w_up[i], dense_w_down[i])
        else:
            i = l // 2
            y = _moe(h.reshape(B * S, D), moe_router[i], moe_w_gate[i], moe_w_up[i], moe_w_down[i])
            x = x + y.reshape(B, S, D)
    return x
```

```python
import functools
import math

import numpy as np
import jax
import jax.numpy as jnp
from jax import lax
from jax.experimental import pallas as pl
from jax.experimental.pallas import tpu as pltpu

HEAD_DIM = 128
DIFF_HALF = HEAD_DIM // 2
DIFF_HEADS = 8
GQA_HEADS = 8
GQA_KV_HEADS = 2
GQA_GROUP = GQA_HEADS // GQA_KV_HEADS
DIFF_WIDTH = DIFF_HEADS * HEAD_DIM
GQA_WIDTH = GQA_HEADS * HEAD_DIM
GQA_KV_WIDTH = GQA_KV_HEADS * HEAD_DIM
GRID_W = 64
ROPE_THETA = 10000.0
ROPE_AXIS_DIM = HEAD_DIM // 2
N_EXPERTS = 8
EPS = 1e-6

LANES = 128
V7X_VMEM_BYTES = 64 * 1024 * 1024
VMEM_LIMIT_BYTES = 52 * 1024 * 1024

BF16 = jnp.bfloat16
F32 = jnp.float32


def _params(*sem):
    return pltpu.CompilerParams(dimension_semantics=sem, vmem_limit_bytes=VMEM_LIMIT_BYTES)


def _rms_rows(xf, g):
    ms = jnp.mean(xf * xf, axis=-1, keepdims=True)
    return (xf * lax.rsqrt(ms + EPS)) * g


def _norm_matmul_kernel(x_ref, g_ref, w_ref, o_ref, n_ref):
    @pl.when(pl.program_id(1) == 0)
    def _():
        n_ref[...] = _rms_rows(x_ref[...], g_ref[...]).astype(n_ref.dtype)

    o_ref[...] = jnp.dot(n_ref[...], w_ref[...], preferred_element_type=F32).astype(o_ref.dtype)


def _norm_matmul(x, g, w, *, tm=1024, tn=512, out_dtype=F32):
    n, d = x.shape
    wd = w.shape[1]
    tm = min(tm, n)
    tn = min(tn, wd)
    return pl.pallas_call(
        _norm_matmul_kernel,
        grid=(n // tm, wd // tn),
        in_specs=[
            pl.BlockSpec((tm, d), lambda i, j: (i, 0)),
            pl.BlockSpec((1, d), lambda i, j: (0, 0)),
            pl.BlockSpec((d, tn), lambda i, j: (0, j)),
        ],
        out_specs=pl.BlockSpec((tm, tn), lambda i, j: (i, j)),
        out_shape=jax.ShapeDtypeStruct((n, wd), out_dtype),
        scratch_shapes=[pltpu.VMEM((tm, d), BF16)],
        compiler_params=_params("parallel", "arbitrary"),
        name="norm_in_proj",
    )(x, g.reshape(1, d), w)


def _prep_kernel(p_ref, cos_ref, sin_ref, dq_ref, dk_ref, gq_ref, gk_ref,
                 aq1_ref, aq2_ref, ak_ref, av_ref, bq_ref, bk_ref, bv_ref):
    lane = lax.broadcasted_iota(jnp.int32, (1, HEAD_DIM), 1)
    lo = lane < DIFF_HALF
    even = (lane & 1) == 0
    cos = cos_ref[...]
    sin = sin_ref[...]

    def half_rms(x, g):
        sq = x * x
        s_lo = jnp.sum(jnp.where(lo, sq, 0.0), axis=-1, keepdims=True)
        s_hi = jnp.sum(jnp.where(lo, 0.0, sq), axis=-1, keepdims=True)
        ms = jnp.where(lo, s_lo, s_hi) * (1.0 / DIFF_HALF)
        return (x * lax.rsqrt(ms + EPS)) * g

    def rope(y):
        sw = jnp.where(even, pltpu.roll(y, HEAD_DIM - 1, 1), pltpu.roll(y, 1, 1))
        return y * cos + sw * sin

    dq = dq_ref[...]
    dk = dk_ref[...]
    gq = gq_ref[...]
    gk = gk_ref[...]
    for h in range(DIFF_HEADS):
        c = h * HEAD_DIM
        q = half_rms(p_ref[:, c:c + HEAD_DIM], dq) * (DIFF_HALF ** -0.5)
        aq1_ref[:, c:c + HEAD_DIM] = jnp.where(lo, q, 0.0).astype(BF16)
        aq2_ref[:, c:c + HEAD_DIM] = jnp.where(lo, 0.0, q).astype(BF16)
        k = half_rms(p_ref[:, DIFF_WIDTH + c:DIFF_WIDTH + c + HEAD_DIM], dk)
        ak_ref[:, c:c + HEAD_DIM] = k.astype(BF16)
        av_ref[:, c:c + HEAD_DIM] = p_ref[:, 2 * DIFF_WIDTH + c:2 * DIFF_WIDTH + c + HEAD_DIM].astype(BF16)
    b0 = 3 * DIFF_WIDTH
    for h in range(GQA_HEADS):
        c = h * HEAD_DIM
        q = rope(_rms_rows(p_ref[:, b0 + c:b0 + c + HEAD_DIM], gq)) * (HEAD_DIM ** -0.5)
        bq_ref[:, c:c + HEAD_DIM] = q.astype(BF16)
    k0 = b0 + GQA_WIDTH
    v0 = k0 + GQA_KV_WIDTH
    for h in range(GQA_KV_HEADS):
        c = h * HEAD_DIM
        k = rope(_rms_rows(p_ref[:, k0 + c:k0 + c + HEAD_DIM], gk))
        bk_ref[:, c:c + HEAD_DIM] = k.astype(BF16)
        bv_ref[:, c:c + HEAD_DIM] = p_ref[:, v0 + c:v0 + c + HEAD_DIM].astype(BF16)


def _prep(proj, cos_t, sin_t, dq, dk, gq, gk, *, seq, tm=512):
    n, w = proj.shape
    tm = min(tm, seq)
    nt = seq // tm
    row = lambda i: (i, 0)
    pos = lambda i: (i % nt, 0)
    fixed = lambda i: (0, 0)
    dq2 = jnp.concatenate([dq, dq]).reshape(1, HEAD_DIM)
    dk2 = jnp.concatenate([dk, dk]).reshape(1, HEAD_DIM)
    shapes = [(n, DIFF_WIDTH)] * 4 + [(n, GQA_WIDTH), (n, GQA_KV_WIDTH), (n, GQA_KV_WIDTH)]
    return pl.pallas_call(
        _prep_kernel,
        grid=(n // tm,),
        in_specs=[
            pl.BlockSpec((tm, w), row),
            pl.BlockSpec((tm, HEAD_DIM), pos),
            pl.BlockSpec((tm, HEAD_DIM), pos),
            pl.BlockSpec((1, HEAD_DIM), fixed),
            pl.BlockSpec((1, HEAD_DIM), fixed),
            pl.BlockSpec((1, HEAD_DIM), fixed),
            pl.BlockSpec((1, HEAD_DIM), fixed),
        ],
        out_specs=[pl.BlockSpec((tm, s[1]), row) for s in shapes],
        out_shape=[jax.ShapeDtypeStruct(s, BF16) for s in shapes],
        compiler_params=_params("parallel"),
        name="head_prep",
    )(proj, cos_t, sin_t, dq2, dk2, gq.reshape(1, HEAD_DIM), gk.reshape(1, HEAD_DIM))


def _rope_tables(seq):
    rows_n = seq // GRID_W
    row = jnp.repeat(jnp.arange(rows_n, dtype=F32), GRID_W, total_repeat_length=seq)
    col = jnp.tile(jnp.arange(GRID_W, dtype=F32), rows_n)
    inv = ROPE_THETA ** (-jnp.arange(0, ROPE_AXIS_DIM, 2, dtype=F32) / ROPE_AXIS_DIM)
    ang = jnp.concatenate([row[:, None] * inv, col[:, None] * inv], axis=-1)
    cos = jnp.repeat(jnp.cos(ang), 2, axis=-1)
    sin = jnp.sin(ang)
    sin_signed = jnp.stack([-sin, sin], axis=-1).reshape(seq, HEAD_DIM)
    return cos, sin_signed


def _qk(q, k):
    return lax.dot_general(q, k, (((1,), (1,)), ((), ())), preferred_element_type=F32)


def _online_update(s, v, m_ref, l_ref, acc_ref, idx):
    m_old = m_ref[idx]
    m_new = jnp.maximum(m_old, jnp.max(s, axis=-1, keepdims=True))
    alpha = jnp.exp(m_old - m_new)
    p = jnp.exp(s - m_new)
    l_ref[idx] = alpha * l_ref[idx] + jnp.sum(p, axis=-1, keepdims=True)
    acc_ref[idx] = alpha * acc_ref[idx] + jnp.dot(p.astype(BF16), v, preferred_element_type=F32)
    m_ref[idx] = m_new


def _diff_attn_kernel(lam_ref, slopes_ref, q1_ref, q2_ref, k_ref, v_ref, g_ref, o_ref,
                      m_ref, l_ref, acc_ref, *, tq, tk, nk, out_scale):
    h = pl.program_id(1)
    qi = pl.program_id(2)
    ki = pl.program_id(3)

    @pl.when(ki == 0)
    def _():
        m_ref[...] = jnp.full(m_ref.shape, -jnp.inf, F32)
        l_ref[...] = jnp.zeros(l_ref.shape, F32)
        acc_ref[...] = jnp.zeros(acc_ref.shape, F32)

    k = k_ref[...]
    v = v_ref[...]
    qpos = qi * tq + lax.broadcasted_iota(jnp.int32, (tq, tk), 0)
    kpos = ki * tk + lax.broadcasted_iota(jnp.int32, (tq, tk), 1)
    bias = jnp.abs(qpos - kpos).astype(F32) * (-slopes_ref[h])
    for c, q_ref in enumerate((q1_ref, q2_ref)):
        s = _qk(q_ref[...], k) + bias
        _online_update(s, v, m_ref, l_ref, acc_ref, c)

    @pl.when(ki == nk - 1)
    def _():
        o = acc_ref[0] / l_ref[0] - lam_ref[0] * (acc_ref[1] / l_ref[1])
        o = _rms_rows(o, g_ref[...]) * out_scale
        o_ref[...] = o.astype(o_ref.dtype)


def _diff_attention(aq1, aq2, ak, av, lam, slopes, subln, *, batch, seq, out_scale, tq=512, tk=512):
    n = aq1.shape[0]
    tq = min(tq, seq)
    tk = min(tk, seq)
    nq, nk = seq // tq, seq // tk
    qmap = lambda b, h, qi, ki: (b * nq + qi, h)
    kmap = lambda b, h, qi, ki: (b * nk + ki, h)
    smem = pl.BlockSpec(memory_space=pltpu.SMEM)
    kern = functools.partial(_diff_attn_kernel, tq=tq, tk=tk, nk=nk, out_scale=out_scale)
    return pl.pallas_call(
        kern,
        grid=(batch, DIFF_HEADS, nq, nk),
        in_specs=[
            smem, smem,
            pl.BlockSpec((tq, HEAD_DIM), qmap),
            pl.BlockSpec((tq, HEAD_DIM), qmap),
            pl.BlockSpec((tk, HEAD_DIM), kmap),
            pl.BlockSpec((tk, HEAD_DIM), kmap),
            pl.BlockSpec((1, HEAD_DIM), lambda b, h, qi, ki: (0, 0)),
        ],
        out_specs=pl.BlockSpec((tq, HEAD_DIM), qmap),
        out_shape=jax.ShapeDtypeStruct((n, DIFF_WIDTH), BF16),
        scratch_shapes=[
            pltpu.VMEM((2, tq, 1), F32),
            pltpu.VMEM((2, tq, 1), F32),
            pltpu.VMEM((2, tq, HEAD_DIM), F32),
        ],
        compiler_params=_params("parallel", "parallel", "parallel", "arbitrary"),
        name="diff_attention",
    )(lam.reshape(1), slopes, aq1, aq2, ak, av, subln.reshape(1, HEAD_DIM))


def _gqa_attn_kernel(q_ref, k_ref, v_ref, o_ref, m_ref, l_ref, acc_ref, *, nk):
    ki = pl.program_id(3)

    @pl.when(ki == 0)
    def _():
        m_ref[...] = jnp.full(m_ref.shape, -jnp.inf, F32)
        l_ref[...] = jnp.zeros(l_ref.shape, F32)
        acc_ref[...] = jnp.zeros(acc_ref.shape, F32)

    k = k_ref[...]
    v = v_ref[...]
    for r in range(GQA_GROUP):
        s = _qk(q_ref[:, r * HEAD_DIM:(r + 1) * HEAD_DIM], k)
        _online_update(s, v, m_ref, l_ref, acc_ref, r)

    @pl.when(ki == nk - 1)
    def _():
        for r in range(GQA_GROUP):
            o_ref[:, r * HEAD_DIM:(r + 1) * HEAD_DIM] = (acc_ref[r] / l_ref[r]).astype(o_ref.dtype)


def _gqa_attention(bq, bk, bv, *, batch, seq, tq=512, tk=512):
    n = bq.shape[0]
    tq = min(tq, seq)
    tk = min(tk, seq)
    nq, nk = seq // tq, seq // tk
    gw = GQA_GROUP * HEAD_DIM
    qmap = lambda b, g, qi, ki: (b * nq + qi, g)
    kmap = lambda b, g, qi, ki: (b * nk + ki, g)
    return pl.pallas_call(
        functools.partial(_gqa_attn_kernel, nk=nk),
        grid=(batch, GQA_KV_HEADS, nq, nk),
        in_specs=[
            pl.BlockSpec((tq, gw), qmap),
            pl.BlockSpec((tk, HEAD_DIM), kmap),
            pl.BlockSpec((tk, HEAD_DIM), kmap),
        ],
        out_specs=pl.BlockSpec((tq, gw), qmap),
        out_shape=jax.ShapeDtypeStruct((n, GQA_WIDTH), BF16),
        scratch_shapes=[
            pltpu.VMEM((GQA_GROUP, tq, 1), F32),
            pltpu.VMEM((GQA_GROUP, tq, 1), F32),
            pltpu.VMEM((GQA_GROUP, tq, HEAD_DIM), F32),
        ],
        compiler_params=_params("parallel", "parallel", "parallel", "arbitrary"),
        name="gqa_attention",
    )(bq, bk, bv)


def _out_proj_kernel(a_ref, b_ref, wa_ref, wb_ref, x_ref, o_ref):
    acc = jnp.dot(a_ref[...], wa_ref[...], preferred_element_type=F32)
    acc = acc + jnp.dot(b_ref[...], wb_ref[...], preferred_element_type=F32)
    o_ref[...] = x_ref[...] + acc


def _out_proj(ao, bo, w_out, x, *, tm=1024, tn=512):
    n, d = x.shape
    tm = min(tm, n)
    tn = min(tn, d)
    wa, wb = ao.shape[1], bo.shape[1]
    return pl.pallas_call(
        _out_proj_kernel,
        grid=(n // tm, d // tn),
        in_specs=[
            pl.BlockSpec((tm, wa), lambda i, j: (i, 0)),
            pl.BlockSpec((tm, wb), lambda i, j: (i, 0)),
            pl.BlockSpec((wa, tn), lambda i, j: (0, j)),
            pl.BlockSpec((wb, tn), lambda i, j: (wa // wb, j)),
            pl.BlockSpec((tm, tn), lambda i, j: (i, j)),
        ],
        out_specs=pl.BlockSpec((tm, tn), lambda i, j: (i, j)),
        out_shape=jax.ShapeDtypeStruct((n, d), F32),
        compiler_params=_params("parallel", "parallel"),
        name="out_proj",
    )(ao, bo, w_out, w_out, x)


def _swiglu_act(g, u):
    return (g * jax.nn.sigmoid(g)) * u


def _ffn_kernel(x_ref, gn_ref, wg_ref, wu_ref, wd_ref, o_ref, h_ref):
    @pl.when(pl.program_id(1) == 0)
    def _():
        xf = x_ref[...]
        h_ref[...] = _rms_rows(xf, gn_ref[...]).astype(h_ref.dtype)
        o_ref[...] = xf

    h = h_ref[...]
    g = jnp.dot(h, wg_ref[...], preferred_element_type=F32)
    u = jnp.dot(h, wu_ref[...], preferred_element_type=F32)
    a = _swiglu_act(g, u).astype(BF16)
    o_ref[...] += jnp.dot(a, wd_ref[...], preferred_element_type=F32)


def _ffn_dense(x, gn, wg, wu, wd, *, tm=512, tf=512):
    n, d = x.shape
    f = wg.shape[1]
    tm = min(tm, n)
    return pl.pallas_call(
        _ffn_kernel,
        grid=(n // tm, f // tf),
        in_specs=[
            pl.BlockSpec((tm, d), lambda i, j: (i, 0)),
            pl.BlockSpec((1, d), lambda i, j: (0, 0)),
            pl.BlockSpec((d, tf), lambda i, j: (0, j)),
            pl.BlockSpec((d, tf), lambda i, j: (0, j)),
            pl.BlockSpec((tf, d), lambda i, j: (j, 0)),
        ],
        out_specs=pl.BlockSpec((tm, d), lambda i, j: (i, 0)),
        out_shape=jax.ShapeDtypeStruct((n, d), F32),
        scratch_shapes=[pltpu.VMEM((tm, d), BF16)],
        compiler_params=_params("parallel", "arbitrary"),
        name="ffn_dense",
    )(x, gn.reshape(1, d), wg, wu, wd)


def _pad_ff(w, axis, mult):
    f = w.shape[axis]
    fp = -(-f // mult) * mult
    if fp == f:
        return w
    pad = [(0, 0)] * w.ndim
    pad[axis] = (0, fp - f)
    return jnp.pad(w, pad)


def _router_kernel(x_ref, gn_ref, r_ref, h_ref, comb_ref):
    h = _rms_rows(x_ref[...], gn_ref[...]).astype(BF16)
    h_ref[...] = h
    logits = jnp.dot(h, r_ref[...], preferred_element_type=F32)
    lane = lax.broadcasted_iota(jnp.int32, logits.shape, 1)
    m1 = jnp.max(logits, axis=-1, keepdims=True)
    i1 = jnp.min(jnp.where(logits == m1, lane, N_EXPERTS), axis=-1, keepdims=True)
    rest = jnp.where(lane == i1, -jnp.inf, logits)
    m2 = jnp.max(rest, axis=-1, keepdims=True)
    i2 = jnp.min(jnp.where(rest == m2, lane, N_EXPERTS), axis=-1, keepdims=True)
    e2 = jnp.exp(m2 - m1)
    den = 1.0 + e2
    comb_ref[...] = jnp.where(lane == i1, 1.0 / den, 0.0) + jnp.where(lane == i2, e2 / den, 0.0)


def _router(x, gn, router, *, tm=512):
    n, d = x.shape
    tm = min(tm, n)
    return pl.pallas_call(
        _router_kernel,
        grid=(n // tm,),
        in_specs=[
            pl.BlockSpec((tm, d), lambda i: (i, 0)),
            pl.BlockSpec((1, d), lambda i: (0, 0)),
            pl.BlockSpec((d, N_EXPERTS), lambda i: (0, 0)),
        ],
        out_specs=[
            pl.BlockSpec((tm, d), lambda i: (i, 0)),
            pl.BlockSpec((tm, N_EXPERTS), lambda i: (i, 0)),
        ],
        out_shape=[
            jax.ShapeDtypeStruct((n, d), BF16),
            jax.ShapeDtypeStruct((n, N_EXPERTS), F32),
        ],
        compiler_params=_params("parallel"),
        name="moe_router",
    )(x, gn.reshape(1, d), router)


def _moe_kernel(h_ref, comb_ref, x_ref, wg_ref, wu_ref, wd_ref, o_ref):
    e = pl.program_id(1)
    j = pl.program_id(2)

    @pl.when((e == 0) & (j == 0))
    def _():
        o_ref[...] = x_ref[...]

    comb = comb_ref[...]
    lane = lax.broadcasted_iota(jnp.int32, comb.shape, 1)
    c = jnp.sum(jnp.where(lane == e, comb, 0.0), axis=-1, keepdims=True)
    h = h_ref[...]
    g = jnp.dot(h, wg_ref[...], preferred_element_type=F32)
    u = jnp.dot(h, wu_ref[...], preferred_element_type=F32)
    a = (_swiglu_act(g, u) * c).astype(BF16)
    o_ref[...] += jnp.dot(a, wd_ref[...], preferred_element_type=F32)


def _moe_dense(h, comb, x, wg, wu, wd, *, tm=512, tf=512):
    n, d = x.shape
    ne, _, f = wg.shape
    tm = min(tm, n)
    return pl.pallas_call(
        _moe_kernel,
        grid=(n // tm, ne, f // tf),
        in_specs=[
            pl.BlockSpec((tm, d), lambda i, e, j: (i, 0)),
            pl.BlockSpec((tm, ne), lambda i, e, j: (i, 0)),
            pl.BlockSpec((tm, d), lambda i, e, j: (i, 0)),
            pl.BlockSpec((None, d, tf), lambda i, e, j: (e, 0, j)),
            pl.BlockSpec((None, d, tf), lambda i, e, j: (e, 0, j)),
            pl.BlockSpec((None, tf, d), lambda i, e, j: (e, j, 0)),
        ],
        out_specs=pl.BlockSpec((tm, d), lambda i, e, j: (i, 0)),
        out_shape=jax.ShapeDtypeStruct((n, d), F32),
        compiler_params=_params("parallel", "arbitrary", "arbitrary"),
        name="moe_experts",
    )(h, comb, x, wg, wu, wd)


def kernel(x, attn_norm, w_in, diff_q_norm, diff_k_norm, diff_lambda, diff_subln, gqa_q_norm, gqa_k_norm, w_out, ffn_norm, dense_w_gate, dense_w_up, dense_w_down, moe_router, moe_w_gate, moe_w_up, moe_w_down):
    batch, seq, d = x.shape
    depth = w_in.shape[0]
    n = batch * seq
    xs = x.reshape(n, d).astype(F32)
    cos_t, sin_t = _rope_tables(seq)
    slopes = jnp.asarray(2.0 ** (-8.0 * (np.arange(DIFF_HEADS) + 1) / DIFF_HEADS), dtype=F32)
    ff_tile = 512

    for l in range(depth):
        proj = _norm_matmul(xs, attn_norm[l], w_in[l].astype(BF16))
        aq1, aq2, ak, av, bq, bk, bv = _prep(
            proj, cos_t, sin_t, diff_q_norm[l], diff_k_norm[l], gqa_q_norm[l], gqa_k_norm[l], seq=seq)
        lam_init = 0.8 - 0.6 * math.exp(-0.3 * l)
        lp = diff_lambda[l].astype(F32)
        lam = jnp.exp(jnp.sum(lp[0] * lp[1])) - jnp.exp(jnp.sum(lp[2] * lp[3])) + lam_init
        ao = _diff_attention(aq1, aq2, ak, av, lam, slopes, diff_subln[l],
                             batch=batch, seq=seq, out_scale=1.0 - lam_init)
        bo = _gqa_attention(bq, bk, bv, batch=batch, seq=seq)
        xs = _out_proj(ao, bo, w_out[l].astype(BF16), xs)

        i = l // 2
        if l % 2 == 0:
            wg = _pad_ff(dense_w_gate[i].astype(BF16), 1, ff_tile)
            wu = _pad_ff(dense_w_up[i].astype(BF16), 1, ff_tile)
            wd = _pad_ff(dense_w_down[i].astype(BF16), 0, ff_tile)
            xs = _ffn_dense(xs, ffn_norm[l], wg, wu, wd, tf=ff_tile)
        else:
            h, comb = _router(xs, ffn_norm[l], moe_router[i].astype(BF16))
            xs = _moe_dense(h, comb, xs, moe_w_gate[i].astype(BF16), moe_w_up[i].astype(BF16),
                            moe_w_down[i].astype(BF16), tf=ff_tile)
    return xs.reshape(batch, seq, d)
```

```python
import functools
import math

import numpy as np
import jax
import jax.numpy as jnp
from jax import lax
from jax.experimental import pallas as pl
from jax.experimental.pallas import tpu as pltpu

HEAD_DIM = 128
DIFF_HALF = HEAD_DIM // 2
DIFF_HEADS = 8
GQA_HEADS = 8
GQA_KV_HEADS = 2
GQA_GROUP = GQA_HEADS // GQA_KV_HEADS
DIFF_WIDTH = DIFF_HEADS * HEAD_DIM
GQA_WIDTH = GQA_HEADS * HEAD_DIM
GQA_KV_WIDTH = GQA_KV_HEADS * HEAD_DIM
GRID_W = 64
ROPE_THETA = 10000.0
ROPE_AXIS_DIM = HEAD_DIM // 2
N_EXPERTS = 8
TOP_K = 2
EPS = 1e-6

LANES = 128
VMEM_LIMIT_BYTES = 52 * 1024 * 1024

BF16 = jnp.bfloat16
F32 = jnp.float32


def _params(*sem):
    return pltpu.CompilerParams(dimension_semantics=sem, vmem_limit_bytes=VMEM_LIMIT_BYTES)


def _rms_rows(xf, g):
    ms = jnp.mean(xf * xf, axis=-1, keepdims=True)
    return (xf * lax.rsqrt(ms + EPS)) * g


def _norm_matmul_kernel(x_ref, g_ref, w_ref, o_ref, n_ref):
    @pl.when(pl.program_id(1) == 0)
    def _():
        n_ref[...] = _rms_rows(x_ref[...], g_ref[...]).astype(n_ref.dtype)

    o_ref[...] = jnp.dot(n_ref[...], w_ref[...], preferred_element_type=F32).astype(o_ref.dtype)


def _norm_matmul(x, g, w, *, tm=1024, tn=512, out_dtype=F32):
    n, d = x.shape
    wd = w.shape[1]
    tm = min(tm, n)
    tn = min(tn, wd)
    return pl.pallas_call(
        _norm_matmul_kernel,
        grid=(n // tm, wd // tn),
        in_specs=[
            pl.BlockSpec((tm, d), lambda i, j: (i, 0)),
            pl.BlockSpec((1, d), lambda i, j: (0, 0)),
            pl.BlockSpec((d, tn), lambda i, j: (0, j)),
        ],
        out_specs=pl.BlockSpec((tm, tn), lambda i, j: (i, j)),
        out_shape=jax.ShapeDtypeStruct((n, wd), out_dtype),
        scratch_shapes=[pltpu.VMEM((tm, d), BF16)],
        compiler_params=_params("parallel", "arbitrary"),
        name="norm_in_proj",
    )(x, g.reshape(1, d), w)


ATTN_TILE = 512
AUG0 = DIFF_HALF
LOG2E = 1.4426950408889634


def _prep_kernel(shift_ref, p_ref, cos_ref, sin_ref, qaug_ref, kaug_ref, dq_ref, dk_ref, gq_ref, gk_ref,
                 aq1_ref, aq2_ref, ak1_ref, ak2_ref, av_ref, bq_ref, bk_ref, bv_ref, *, slopes):
    lane = lax.broadcasted_iota(jnp.int32, (1, HEAD_DIM), 1)
    lo = lane < DIFF_HALF
    even = (lane & 1) == 0
    cos = cos_ref[...]
    sin = sin_ref[...]
    qaug = qaug_ref[...]
    kaug = kaug_ref[...]
    tail = jnp.where(lane == AUG0 + 4, 1.0, 0.0) + jnp.where(lane == AUG0 + 5, -shift_ref[0], 0.0)

    def half_rms(x, g):
        sq = x * x
        s_lo = jnp.sum(jnp.where(lo, sq, 0.0), axis=-1, keepdims=True)
        s_hi = jnp.sum(jnp.where(lo, 0.0, sq), axis=-1, keepdims=True)
        ms = jnp.where(lo, s_lo, s_hi) * (1.0 / DIFF_HALF)
        return (x * lax.rsqrt(ms + EPS)) * g

    def rope(y):
        sw = jnp.where(even, pltpu.roll(y, HEAD_DIM - 1, 1), pltpu.roll(y, 1, 1))
        return y * cos + sw * sin

    dq = dq_ref[...]
    dk = dk_ref[...]
    gq = gq_ref[...]
    gk = gk_ref[...]
    for h in range(DIFF_HEADS):
        c = h * HEAD_DIM
        sl = slice(c, c + HEAD_DIM)
        q = half_rms(p_ref[:, sl], dq) * (DIFF_HALF ** -0.5)
        qa = qaug * slopes[h] + tail
        aq1_ref[:, sl] = jnp.where(lo, q, qa).astype(BF16)
        aq2_ref[:, sl] = jnp.where(lo, pltpu.roll(q, DIFF_HALF, 1), qa).astype(BF16)
        k = half_rms(p_ref[:, DIFF_WIDTH + c:DIFF_WIDTH + c + HEAD_DIM], dk)
        ak1_ref[:, sl] = jnp.where(lo, k, kaug).astype(BF16)
        ak2_ref[:, sl] = jnp.where(lo, pltpu.roll(k, DIFF_HALF, 1), kaug).astype(BF16)
        av_ref[:, sl] = p_ref[:, 2 * DIFF_WIDTH + c:2 * DIFF_WIDTH + c + HEAD_DIM].astype(BF16)
    b0 = 3 * DIFF_WIDTH
    for h in range(GQA_HEADS):
        c = h * HEAD_DIM
        q = rope(_rms_rows(p_ref[:, b0 + c:b0 + c + HEAD_DIM], gq)) * (HEAD_DIM ** -0.5 * LOG2E)
        bq_ref[:, c:c + HEAD_DIM] = q.astype(BF16)
    k0 = b0 + GQA_WIDTH
    v0 = k0 + GQA_KV_WIDTH
    for h in range(GQA_KV_HEADS):
        c = h * HEAD_DIM
        k = rope(_rms_rows(p_ref[:, k0 + c:k0 + c + HEAD_DIM], gk))
        bk_ref[:, c:c + HEAD_DIM] = k.astype(BF16)
        bv_ref[:, c:c + HEAD_DIM] = p_ref[:, v0 + c:v0 + c + HEAD_DIM].astype(BF16)


def _prep(proj, tables, shift, dq, dk, gq, gk, *, seq, slopes, tm=512):
    n, w = proj.shape
    tm = min(tm, seq)
    nt = seq // tm
    row = lambda i: (i, 0)
    pos = lambda i: (i % nt, 0)
    fixed = lambda i: (0, 0)
    dq2 = jnp.concatenate([dq, dq]).reshape(1, HEAD_DIM)
    dk2 = jnp.concatenate([dk, dk]).reshape(1, HEAD_DIM)
    shapes = [(n, DIFF_WIDTH)] * 5 + [(n, GQA_WIDTH), (n, GQA_KV_WIDTH), (n, GQA_KV_WIDTH)]
    return pl.pallas_call(
        functools.partial(_prep_kernel, slopes=slopes),
        grid=(n // tm,),
        in_specs=[pl.BlockSpec(memory_space=pltpu.SMEM), pl.BlockSpec((tm, w), row)]
        + [pl.BlockSpec((tm, HEAD_DIM), pos)] * 4
        + [pl.BlockSpec((1, HEAD_DIM), fixed)] * 4,
        out_specs=[pl.BlockSpec((tm, s[1]), row) for s in shapes],
        out_shape=[jax.ShapeDtypeStruct(s, BF16) for s in shapes],
        compiler_params=_params("parallel"),
        name="head_prep",
    )(shift.reshape(1), proj, *tables, dq2, dk2, gq.reshape(1, HEAD_DIM), gk.reshape(1, HEAD_DIM))


def _position_tables(seq, tile):
    rows_n = seq // GRID_W
    row = jnp.repeat(jnp.arange(rows_n, dtype=F32), GRID_W, total_repeat_length=seq)
    col = jnp.tile(jnp.arange(GRID_W, dtype=F32), rows_n)
    inv = ROPE_THETA ** (-jnp.arange(0, ROPE_AXIS_DIM, 2, dtype=F32) / ROPE_AXIS_DIM)
    ang = jnp.concatenate([row[:, None] * inv, col[:, None] * inv], axis=-1)
    cos = jnp.repeat(jnp.cos(ang), 2, axis=-1)
    sin = jnp.sin(ang)
    sin_signed = jnp.stack([-sin, sin], axis=-1).reshape(seq, HEAD_DIM)
    loc = np.arange(seq) % tile
    hi = (loc // 16).astype(np.float32)
    lo = (loc % 16).astype(np.float32)
    qaug = np.zeros((seq, HEAD_DIM), np.float32)
    kaug = np.zeros((seq, HEAD_DIM), np.float32)
    qaug[:, AUG0 + 0] = 16.0
    qaug[:, AUG0 + 1] = 1.0
    qaug[:, AUG0 + 2] = -16.0 * hi
    qaug[:, AUG0 + 3] = -lo
    kaug[:, AUG0 + 0] = hi
    kaug[:, AUG0 + 1] = lo
    kaug[:, AUG0 + 2:AUG0 + 6] = 1.0
    return cos, sin_signed, jnp.asarray(qaug), jnp.asarray(kaug)


SHIFT_MAX = 40.0
UNDERFLOW = 106.0


def _qk(q, k):
    return lax.dot_general(q, k, (((1,), (1,)), ((), ())), preferred_element_type=F32)


def _lane_fold(p):
    out = p[:, :LANES]
    for c in range(1, p.shape[1] // LANES):
        out = out + p[:, c * LANES:(c + 1) * LANES]
    return out


def _online_update(s, v, m_ref, l_ref, acc_ref, idx, exp_fn):
    m_old = m_ref[idx]
    m_new = jnp.maximum(m_old, jnp.max(s, axis=-1, keepdims=True))
    alpha = exp_fn(m_old - m_new)
    p = exp_fn(s - m_new)
    l_ref[idx] = alpha * l_ref[idx] + jnp.sum(p, axis=-1, keepdims=True)
    acc_ref[idx] = alpha * acc_ref[idx] + jnp.dot(p.astype(BF16), v, preferred_element_type=F32)
    m_ref[idx] = m_new


def _shifted_update(p, v, l_ref, acc_ref, idx):
    l_ref[idx] += _lane_fold(p)
    acc_ref[idx] += jnp.dot(p.astype(BF16), v, preferred_element_type=F32)


def _row_total(l_ref, idx, stable):
    return l_ref[idx] if stable else jnp.sum(l_ref[idx], axis=-1, keepdims=True)


def _diff_attn_kernel(lam_ref, q1_ref, q2_ref, k1_ref, k2_ref, v_ref, g_ref, o_ref, *scratch,
                      slope, t, nk, nj, window, out_scale, stable):
    if stable:
        m_ref, l_ref, acc_ref = scratch
    else:
        l_ref, acc_ref = scratch
    qi = pl.program_id(1)
    j = pl.program_id(2)
    d = (window - j) if window is not None else (qi - j)
    ki = qi - d

    @pl.when(j == 0)
    def _():
        if stable:
            m_ref[...] = jnp.full(m_ref.shape, -jnp.inf, F32)
        l_ref[...] = jnp.zeros(l_ref.shape, F32)
        acc_ref[...] = jnp.zeros(acc_ref.shape, F32)

    def tile(bias):
        lane = lax.broadcasted_iota(jnp.int32, (1, HEAD_DIM), 1)
        sig = jnp.where(d > 0, 1.0, jnp.where(d < 0, -1.0, 0.0))
        tval = (-slope * t) * jnp.abs(d).astype(F32)
        mult = jnp.where(lane < AUG0, 1.0,
                         jnp.where(lane < AUG0 + 4, sig, jnp.where(lane == AUG0 + 4, tval, 1.0))).astype(BF16)
        v = v_ref[...]
        for c, (q_ref, k_ref) in enumerate(((q1_ref, k1_ref), (q2_ref, k2_ref))):
            s = _qk(q_ref[...] * mult, k_ref[...])
            if bias is not None:
                s = s + bias
            if stable:
                _online_update(s, v, m_ref, l_ref, acc_ref, c, jnp.exp)
            else:
                _shifted_update(jnp.exp(s), v, l_ref, acc_ref, c)

    @pl.when((d != 0) & (ki >= 0) & (ki < nk))
    def _():
        tile(None)

    @pl.when(d == 0)
    def _():
        r = lax.broadcasted_iota(jnp.int32, (t, t), 0)
        c = lax.broadcasted_iota(jnp.int32, (t, t), 1)
        tile(jnp.abs(r - c).astype(F32) * (-slope))

    @pl.when(j == nj - 1)
    def _():
        o = acc_ref[0] / _row_total(l_ref, 0, stable) - lam_ref[0] * (acc_ref[1] / _row_total(l_ref, 1, stable))
        o = _rms_rows(o, g_ref[...]) * out_scale
        o_ref[...] = o.astype(o_ref.dtype)


def _diff_window(slope, tile, nk):
    w = int(math.floor((UNDERFLOW / slope - 1.0) / tile)) + 1
    return None if 2 * w + 1 >= nk else w


def _diff_attention_head(h, aq1, aq2, ak1, ak2, av, lam, subln, *, batch, seq, out_scale, stable):
    n = aq1.shape[0]
    t = min(ATTN_TILE, seq)
    nq = nk = seq // t
    slope = 2.0 ** (-8.0 * (h + 1) / DIFF_HEADS)
    window = None if stable else _diff_window(slope, t, nk)
    nj = nk if window is None else 2 * window + 1
    qmap = lambda b, qi, j: (b * nq + qi, h)
    if window is None:
        kmap = lambda b, qi, j: (b * nk + j, h)
    else:
        kmap = lambda b, qi, j: (b * nk + jnp.clip(qi - window + j, 0, nk - 1), h)
    kern = functools.partial(_diff_attn_kernel, slope=slope, t=t, nk=nk, nj=nj, window=window,
                             out_scale=out_scale, stable=stable)
    stat = [pltpu.VMEM((2, t, 1), F32)] * 2 if stable else [pltpu.VMEM((2, t, LANES), F32)]
    return pl.pallas_call(
        kern,
        grid=(batch, nq, nj),
        in_specs=[
            pl.BlockSpec(memory_space=pltpu.SMEM),
            pl.BlockSpec((t, HEAD_DIM), qmap),
            pl.BlockSpec((t, HEAD_DIM), qmap),
            pl.BlockSpec((t, HEAD_DIM), kmap),
            pl.BlockSpec((t, HEAD_DIM), kmap),
            pl.BlockSpec((t, HEAD_DIM), kmap),
            pl.BlockSpec((1, HEAD_DIM), lambda b, qi, j: (0, 0)),
        ],
        out_specs=pl.BlockSpec((t, HEAD_DIM), lambda b, qi, j: (b * nq + qi, 0)),
        out_shape=jax.ShapeDtypeStruct((n, HEAD_DIM), BF16),
        scratch_shapes=stat + [pltpu.VMEM((2, t, HEAD_DIM), F32)],
        compiler_params=_params("parallel", "parallel", "arbitrary"),
        name=f"diff_attention_h{h}" + ("_stable" if stable else ""),
    )(lam.reshape(1), aq1, aq2, ak1, ak2, av, subln.reshape(1, HEAD_DIM))


def _diff_attention(aq1, aq2, ak1, ak2, av, lam, subln, bound_ok, *, batch, seq, out_scale):
    def run(stable):
        def f(aq1, aq2, ak1, ak2, av, lam, subln):
            heads = [_diff_attention_head(h, aq1, aq2, ak1, ak2, av, lam, subln, batch=batch, seq=seq,
                                          out_scale=out_scale, stable=stable) for h in range(DIFF_HEADS)]
            return jnp.concatenate(heads, axis=-1)
        return f
    return lax.cond(bound_ok, run(False), run(True), aq1, aq2, ak1, ak2, av, lam, subln)


def _gqa_attn_kernel(shift_ref, q_ref, k_ref, v_ref, o_ref, *scratch, nk, stable):
    if stable:
        m_ref, l_ref, acc_ref = scratch
    else:
        l_ref, acc_ref = scratch
    ki = pl.program_id(3)

    @pl.when(ki == 0)
    def _():
        if stable:
            m_ref[...] = jnp.full(m_ref.shape, -jnp.inf, F32)
        l_ref[...] = jnp.zeros(l_ref.shape, F32)
        acc_ref[...] = jnp.zeros(acc_ref.shape, F32)

    k = k_ref[...]
    v = v_ref[...]
    for r in range(GQA_GROUP):
        s = _qk(q_ref[:, r * HEAD_DIM:(r + 1) * HEAD_DIM], k)
        if stable:
            _online_update(s, v, m_ref, l_ref, acc_ref, r, jnp.exp2)
        else:
            _shifted_update(jnp.exp2(s - shift_ref[0]), v, l_ref, acc_ref, r)

    @pl.when(ki == nk - 1)
    def _():
        for r in range(GQA_GROUP):
            o = acc_ref[r] / _row_total(l_ref, r, stable)
            o_ref[:, r * HEAD_DIM:(r + 1) * HEAD_DIM] = o.astype(o_ref.dtype)


def _gqa_attention_call(shift, bq, bk, bv, *, batch, seq, stable):
    n = bq.shape[0]
    t = min(ATTN_TILE, seq)
    nq = nk = seq // t
    gw = GQA_GROUP * HEAD_DIM
    qmap = lambda b, g, qi, ki: (b * nq + qi, g)
    kmap = lambda b, g, qi, ki: (b * nk + ki, g)
    stat = [pltpu.VMEM((GQA_GROUP, t, 1), F32)] * 2 if stable else [pltpu.VMEM((GQA_GROUP, t, LANES), F32)]
    return pl.pallas_call(
        functools.partial(_gqa_attn_kernel, nk=nk, stable=stable),
        grid=(batch, GQA_KV_HEADS, nq, nk),
        in_specs=[
            pl.BlockSpec(memory_space=pltpu.SMEM),
            pl.BlockSpec((t, gw), qmap),
            pl.BlockSpec((t, HEAD_DIM), kmap),
            pl.BlockSpec((t, HEAD_DIM), kmap),
        ],
        out_specs=pl.BlockSpec((t, gw), qmap),
        out_shape=jax.ShapeDtypeStruct((n, GQA_WIDTH), BF16),
        scratch_shapes=stat + [pltpu.VMEM((GQA_GROUP, t, HEAD_DIM), F32)],
        compiler_params=_params("parallel", "parallel", "parallel", "arbitrary"),
        name="gqa_attention" + ("_stable" if stable else ""),
    )(shift.reshape(1), bq, bk, bv)


def _gqa_attention(bq, bk, bv, bound, *, batch, seq):
    shift = bound * LOG2E
    return lax.cond(
        bound <= SHIFT_MAX,
        functools.partial(_gqa_attention_call, batch=batch, seq=seq, stable=False),
        functools.partial(_gqa_attention_call, batch=batch, seq=seq, stable=True),
        shift, bq, bk, bv)


def _out_proj_kernel(a_ref, b_ref, wa_ref, wb_ref, x_ref, o_ref):
    acc = jnp.dot(a_ref[...], wa_ref[...], preferred_element_type=F32)
    acc = acc + jnp.dot(b_ref[...], wb_ref[...], preferred_element_type=F32)
    o_ref[...] = x_ref[...] + acc


def _out_proj(ao, bo, w_out, x, *, tm=1024, tn=512):
    n, d = x.shape
    tm = min(tm, n)
    tn = min(tn, d)
    wa, wb = ao.shape[1], bo.shape[1]
    return pl.pallas_call(
        _out_proj_kernel,
        grid=(n // tm, d // tn),
        in_specs=[
            pl.BlockSpec((tm, wa), lambda i, j: (i, 0)),
            pl.BlockSpec((tm, wb), lambda i, j: (i, 0)),
            pl.BlockSpec((wa, tn), lambda i, j: (0, j)),
            pl.BlockSpec((wb, tn), lambda i, j: (wa // wb, j)),
            pl.BlockSpec((tm, tn), lambda i, j: (i, j)),
        ],
        out_specs=pl.BlockSpec((tm, tn), lambda i, j: (i, j)),
        out_shape=jax.ShapeDtypeStruct((n, d), F32),
        compiler_params=_params("parallel", "parallel"),
        name="out_proj",
    )(ao, bo, w_out, w_out, x)


def _swiglu_act(g, u):
    return (g * jax.nn.sigmoid(g)) * u


def _ffn_kernel(x_ref, gn_ref, wg_ref, wu_ref, wd_ref, o_ref, h_ref):
    @pl.when(pl.program_id(1) == 0)
    def _():
        xf = x_ref[...]
        h_ref[...] = _rms_rows(xf, gn_ref[...]).astype(h_ref.dtype)
        o_ref[...] = xf

    h = h_ref[...]
    g = jnp.dot(h, wg_ref[...], preferred_element_type=F32)
    u = jnp.dot(h, wu_ref[...], preferred_element_type=F32)
    a = _swiglu_act(g, u).astype(BF16)
    o_ref[...] += jnp.dot(a, wd_ref[...], preferred_element_type=F32)


def _ffn_dense(x, gn, wg, wu, wd, *, tm=512, tf=512):
    n, d = x.shape
    f = wg.shape[1]
    tm = min(tm, n)
    return pl.pallas_call(
        _ffn_kernel,
        grid=(n // tm, f // tf),
        in_specs=[
            pl.BlockSpec((tm, d), lambda i, j: (i, 0)),
            pl.BlockSpec((1, d), lambda i, j: (0, 0)),
            pl.BlockSpec((d, tf), lambda i, j: (0, j)),
            pl.BlockSpec((d, tf), lambda i, j: (0, j)),
            pl.BlockSpec((tf, d), lambda i, j: (j, 0)),
        ],
        out_specs=pl.BlockSpec((tm, d), lambda i, j: (i, 0)),
        out_shape=jax.ShapeDtypeStruct((n, d), F32),
        scratch_shapes=[pltpu.VMEM((tm, d), BF16)],
        compiler_params=_params("parallel", "arbitrary"),
        name="ffn_dense",
    )(x, gn.reshape(1, d), wg, wu, wd)


def _pad_ff(w, axis, mult):
    f = w.shape[axis]
    fp = -(-f // mult) * mult
    if fp == f:
        return w
    pad = [(0, 0)] * w.ndim
    pad[axis] = (0, fp - f)
    return jnp.pad(w, pad)


def _router_kernel(x_ref, gn_ref, r_ref, h_ref, idx_ref, gate_ref):
    h = _rms_rows(x_ref[...], gn_ref[...])
    h_ref[...] = h
    logits = jnp.dot(h.astype(BF16), r_ref[...], preferred_element_type=F32)
    lane = lax.broadcasted_iota(jnp.int32, logits.shape, 1)
    m1 = jnp.max(logits, axis=-1, keepdims=True)
    i1 = jnp.min(jnp.where(logits == m1, lane, N_EXPERTS), axis=-1, keepdims=True)
    rest = jnp.where(lane == i1, -jnp.inf, logits)
    m2 = jnp.max(rest, axis=-1, keepdims=True)
    i2 = jnp.min(jnp.where(rest == m2, lane, N_EXPERTS), axis=-1, keepdims=True)
    e2 = jnp.exp(m2 - m1)
    den = 1.0 + e2
    first = lax.broadcasted_iota(jnp.int32, idx_ref.shape, 1) == 0
    idx_ref[...] = jnp.where(first, i1, i2)
    gate_ref[...] = jnp.where(first, 1.0 / den, e2 / den)


def _router(x, gn, router, *, tm=512):
    n, d = x.shape
    tm = min(tm, n)
    return pl.pallas_call(
        _router_kernel,
        grid=(n // tm,),
        in_specs=[
            pl.BlockSpec((tm, d), lambda i: (i, 0)),
            pl.BlockSpec((1, d), lambda i: (0, 0)),
            pl.BlockSpec((d, N_EXPERTS), lambda i: (0, 0)),
        ],
        out_specs=[
            pl.BlockSpec((tm, d), lambda i: (i, 0)),
            pl.BlockSpec((tm, TOP_K), lambda i: (i, 0)),
            pl.BlockSpec((tm, TOP_K), lambda i: (i, 0)),
        ],
        out_shape=[
            jax.ShapeDtypeStruct((n, d), F32),
            jax.ShapeDtypeStruct((n, TOP_K), jnp.int32),
            jax.ShapeDtypeStruct((n, TOP_K), F32),
        ],
        compiler_params=_params("parallel"),
        name="moe_router",
    )(x, gn.reshape(1, d), router)


def _row_copy(idx_ref, src_ref, dst_ref, sem, r):
    return pltpu.make_async_copy(src_ref.at[pl.ds(idx_ref[r], 1)], dst_ref.at[pl.ds(r, 1)], sem)


def _gather_rows(idx_ref, src_ref, dst_ref, sem, rows):
    def start(r, carry):
        _row_copy(idx_ref, src_ref, dst_ref, sem, r).start()
        return carry

    def wait(r, carry):
        _row_copy(idx_ref, src_ref, dst_ref, sem, r).wait()
        return carry

    lax.fori_loop(0, rows, start, 0)
    lax.fori_loop(0, rows, wait, 0)


def _gather_kernel(idx_ref, src_ref, o_ref, sem, *, rows):
    _gather_rows(idx_ref, src_ref, o_ref, sem, rows)


def _gather(src, idx, *, rows=512):
    r = idx.shape[0]
    d = src.shape[1]
    rows = min(rows, r)
    return pl.pallas_call(
        functools.partial(_gather_kernel, rows=rows),
        grid=(r // rows,),
        in_specs=[
            pl.BlockSpec((rows,), lambda i: (i,), memory_space=pltpu.SMEM),
            pl.BlockSpec(memory_space=pl.ANY),
        ],
        out_specs=pl.BlockSpec((rows, d), lambda i: (i, 0)),
        out_shape=jax.ShapeDtypeStruct((r, d), src.dtype),
        scratch_shapes=[pltpu.SemaphoreType.DMA],
        compiler_params=_params("arbitrary"),
        name="moe_gather",
    )(idx, src)


def _moe_ffn_kernel(te_ref, nu_ref, h_ref, gate_ref, wg_ref, wu_ref, wd_ref, o_ref, hb_ref):
    i = pl.program_id(0)
    j = pl.program_id(1)
    used = i < nu_ref[0]

    @pl.when(j == 0)
    def _():
        hb_ref[...] = h_ref[...].astype(BF16)
        o_ref[...] = jnp.zeros(o_ref.shape, F32)

    @pl.when(used)
    def _():
        h = hb_ref[...]
        g = jnp.dot(h, wg_ref[...], preferred_element_type=F32)
        u = jnp.dot(h, wu_ref[...], preferred_element_type=F32)
        a = (_swiglu_act(g, u) * gate_ref[...]).astype(BF16)
        o_ref[...] += jnp.dot(a, wd_ref[...], preferred_element_type=F32)


def _moe_ffn(tile_expert, n_used, h_sorted, gate_sorted, wg, wu, wd, *, tm, tf=512):
    r, d = h_sorted.shape
    f = wg.shape[2]
    nf = f // tf
    jj = lambda i, j, te, nu: jnp.where(i < nu[0], j, nf - 1)
    return pl.pallas_call(
        _moe_ffn_kernel,
        grid_spec=pltpu.PrefetchScalarGridSpec(
            num_scalar_prefetch=2,
            grid=(r // tm, nf),
            in_specs=[
                pl.BlockSpec((tm, d), lambda i, j, te, nu: (i, 0)),
                pl.BlockSpec((tm, 1), lambda i, j, te, nu: (i, 0)),
                pl.BlockSpec((None, d, tf), lambda i, j, te, nu: (te[i], 0, jj(i, j, te, nu))),
                pl.BlockSpec((None, d, tf), lambda i, j, te, nu: (te[i], 0, jj(i, j, te, nu))),
                pl.BlockSpec((None, tf, d), lambda i, j, te, nu: (te[i], jj(i, j, te, nu), 0)),
            ],
            out_specs=pl.BlockSpec((tm, d), lambda i, j, te, nu: (i, 0)),
            scratch_shapes=[pltpu.VMEM((tm, d), BF16)],
        ),
        out_shape=jax.ShapeDtypeStruct((r, d), F32),
        compiler_params=_params("arbitrary", "arbitrary"),
        name="moe_experts",
    )(tile_expert, n_used, h_sorted, gate_sorted, wg, wu, wd)


def _combine_kernel(idx_ref, y_ref, x_ref, o_ref, buf_ref, sem, *, tc):
    _gather_rows(idx_ref, y_ref, buf_ref, sem, TOP_K * tc)
    o_ref[...] = x_ref[...] + buf_ref[pl.ds(0, tc), :] + buf_ref[pl.ds(tc, tc), :]


def _combine(y_sorted, pos, x, *, tc=256):
    n, d = x.shape
    tc = min(tc, n)
    idx = pos.reshape(n // tc, tc, TOP_K).transpose(0, 2, 1).reshape(-1)
    return pl.pallas_call(
        functools.partial(_combine_kernel, tc=tc),
        grid=(n // tc,),
        in_specs=[
            pl.BlockSpec((TOP_K * tc,), lambda i: (i,), memory_space=pltpu.SMEM),
            pl.BlockSpec(memory_space=pl.ANY),
            pl.BlockSpec((tc, d), lambda i: (i, 0)),
        ],
        out_specs=pl.BlockSpec((tc, d), lambda i: (i, 0)),
        out_shape=jax.ShapeDtypeStruct((n, d), F32),
        scratch_shapes=[pltpu.VMEM((TOP_K * tc, d), F32), pltpu.SemaphoreType.DMA],
        compiler_params=_params("arbitrary"),
        name="moe_combine",
    )(idx, y_sorted, x)


def _route_plan(idx, gate, tm):
    n = idx.shape[0]
    a = n * TOP_K
    rows = a + N_EXPERTS * tm
    e_flat = idx.reshape(a)
    order = jnp.argsort(e_flat, stable=True).astype(jnp.int32)
    counts = jnp.sum(e_flat[:, None] == jnp.arange(N_EXPERTS)[None, :], axis=0).astype(jnp.int32)
    padded = ((counts + tm - 1) // tm) * tm
    pad_start = jnp.cumsum(padded) - padded
    raw_start = jnp.cumsum(counts) - counts
    e_sorted = e_flat[order]
    row_of_sorted = pad_start[e_sorted] + (jnp.arange(a, dtype=jnp.int32) - raw_start[e_sorted])
    pos = jnp.zeros((a,), jnp.int32).at[order].set(row_of_sorted)
    row_token = jnp.zeros((rows,), jnp.int32).at[row_of_sorted].set(order // TOP_K)
    row_gate = jnp.zeros((rows,), F32).at[pos].set(gate.reshape(a))
    tile_start = jnp.arange(rows // tm, dtype=jnp.int32) * tm
    pad_end = jnp.cumsum(padded)
    tile_expert = jnp.minimum(jnp.sum(tile_start[:, None] >= pad_end[None, :], axis=1), N_EXPERTS - 1)
    n_used = (pad_end[-1] // tm).reshape(1)
    return (row_token, row_gate.reshape(rows, 1), pos.reshape(n, TOP_K),
            tile_expert.astype(jnp.int32), n_used.astype(jnp.int32))


def _moe(x, gn, router, wg, wu, wd, *, tm=512):
    n = x.shape[0]
    tm = min(tm, n)
    h, idx, gate = _router(x, gn, router)
    row_token, row_gate, pos, tile_expert, n_used = _route_plan(idx, gate, tm)
    h_sorted = _gather(h, row_token, rows=tm)
    y_sorted = _moe_ffn(tile_expert, n_used, h_sorted, row_gate, wg, wu, wd, tm=tm)
    return _combine(y_sorted, pos, x)


def kernel(x, attn_norm, w_in, diff_q_norm, diff_k_norm, diff_lambda, diff_subln, gqa_q_norm, gqa_k_norm, w_out, ffn_norm, dense_w_gate, dense_w_up, dense_w_down, moe_router, moe_w_gate, moe_w_up, moe_w_down):
    batch, seq, d = x.shape
    depth = w_in.shape[0]
    n = batch * seq
    xs = x.reshape(n, d).astype(F32)
    tables = _position_tables(seq, min(ATTN_TILE, seq))
    slopes = tuple(2.0 ** (-8.0 * (h + 1) / DIFF_HEADS) for h in range(DIFF_HEADS))
    ff_tile = 512
    bound_margin = 1.02

    for l in range(depth):
        proj = _norm_matmul(xs, attn_norm[l], w_in[l].astype(BF16))
        amax = lambda g: jnp.max(jnp.abs(g.astype(F32)))
        diff_bound = amax(diff_q_norm[l]) * amax(diff_k_norm[l]) * (DIFF_HALF ** 0.5 * bound_margin)
        gqa_bound = amax(gqa_q_norm[l]) * amax(gqa_k_norm[l]) * (HEAD_DIM ** 0.5 * bound_margin)
        diff_ok = diff_bound <= SHIFT_MAX
        diff_shift = jnp.where(diff_ok, diff_bound, 0.0)
        aq1, aq2, ak1, ak2, av, bq, bk, bv = _prep(
            proj, tables, diff_shift, diff_q_norm[l], diff_k_norm[l], gqa_q_norm[l], gqa_k_norm[l],
            seq=seq, slopes=slopes)
        lam_init = 0.8 - 0.6 * math.exp(-0.3 * l)
        lp = diff_lambda[l].astype(F32)
        lam = jnp.exp(jnp.sum(lp[0] * lp[1])) - jnp.exp(jnp.sum(lp[2] * lp[3])) + lam_init
        ao = _diff_attention(aq1, aq2, ak1, ak2, av, lam, diff_subln[l], diff_ok,
                             batch=batch, seq=seq, out_scale=1.0 - lam_init)
        bo = _gqa_attention(bq, bk, bv, gqa_bound, batch=batch, seq=seq)
        xs = _out_proj(ao, bo, w_out[l].astype(BF16), xs)

        i = l // 2
        if l % 2 == 0:
            wg = _pad_ff(dense_w_gate[i].astype(BF16), 1, ff_tile)
            wu = _pad_ff(dense_w_up[i].astype(BF16), 1, ff_tile)
            wd = _pad_ff(dense_w_down[i].astype(BF16), 0, ff_tile)
            xs = _ffn_dense(xs, ffn_norm[l], wg, wu, wd, tf=ff_tile)
        else:
            xs = _moe(xs, ffn_norm[l], moe_router[i].astype(BF16), moe_w_gate[i].astype(BF16),
                      moe_w_up[i].astype(BF16), moe_w_down[i].astype(BF16))
    return xs.reshape(batch, seq, d)
```

```python
import functools
import math

import numpy as np
import jax
import jax.numpy as jnp
from jax import lax
from jax.experimental import pallas as pl
from jax.experimental.pallas import tpu as pltpu

HEAD_DIM = 128
DIFF_HALF = HEAD_DIM // 2
DIFF_HEADS = 8
GQA_HEADS = 8
GQA_KV_HEADS = 2
GQA_GROUP = GQA_HEADS // GQA_KV_HEADS
DIFF_WIDTH = DIFF_HEADS * HEAD_DIM
GQA_WIDTH = GQA_HEADS * HEAD_DIM
GQA_KV_WIDTH = GQA_KV_HEADS * HEAD_DIM
GRID_W = 64
ROPE_THETA = 10000.0
ROPE_AXIS_DIM = HEAD_DIM // 2
N_EXPERTS = 8
TOP_K = 2
EPS = 1e-6

LANES = 128
VMEM_LIMIT_BYTES = 52 * 1024 * 1024

BF16 = jnp.bfloat16
F32 = jnp.float32


def _params(*sem):
    return pltpu.CompilerParams(dimension_semantics=sem, vmem_limit_bytes=VMEM_LIMIT_BYTES)


def _rms_rows(xf, g):
    ms = jnp.mean(xf * xf, axis=-1, keepdims=True)
    return (xf * lax.rsqrt(ms + EPS)) * g


def _norm_matmul_kernel(x_ref, g_ref, w_ref, o_ref, n_ref):
    @pl.when(pl.program_id(1) == 0)
    def _():
        n_ref[...] = _rms_rows(x_ref[...], g_ref[...]).astype(n_ref.dtype)

    o_ref[...] = jnp.dot(n_ref[...], w_ref[...], preferred_element_type=F32).astype(o_ref.dtype)


def _norm_matmul(x, g, w, *, tm=1024, tn=512, out_dtype=F32):
    n, d = x.shape
    wd = w.shape[1]
    tm = min(tm, n)
    tn = min(tn, wd)
    return pl.pallas_call(
        _norm_matmul_kernel,
        grid=(n // tm, wd // tn),
        in_specs=[
            pl.BlockSpec((tm, d), lambda i, j: (i, 0)),
            pl.BlockSpec((1, d), lambda i, j: (0, 0)),
            pl.BlockSpec((d, tn), lambda i, j: (0, j)),
        ],
        out_specs=pl.BlockSpec((tm, tn), lambda i, j: (i, j)),
        out_shape=jax.ShapeDtypeStruct((n, wd), out_dtype),
        scratch_shapes=[pltpu.VMEM((tm, d), BF16)],
        compiler_params=_params("parallel", "arbitrary"),
        name="norm_in_proj",
    )(x, g.reshape(1, d), w)


ATTN_TILE = 512
DIFF_QSUB = 2
GQA_QTILE = 1024
GQA_KTILE = 1024
AUG0 = DIFF_HALF
LOG2E = 1.4426950408889634


def _prep_kernel(shift_ref, p_ref, cos_ref, sin_ref, qaug_ref, kaug_ref, dq_ref, dk_ref, gq_ref, gk_ref,
                 aq1_ref, aq2_ref, ak1_ref, ak2_ref, av_ref, bq_ref, bk_ref, bv_ref, *, slopes):
    lane = lax.broadcasted_iota(jnp.int32, (1, HEAD_DIM), 1)
    lo = lane < DIFF_HALF
    even = (lane & 1) == 0
    cos = cos_ref[...]
    sin = sin_ref[...]
    qaug = qaug_ref[...]
    kaug = kaug_ref[...]
    tail = jnp.where(lane == AUG0 + 4, 1.0, 0.0) + jnp.where(lane == AUG0 + 5, -shift_ref[0], 0.0)

    def half_rms(x, g):
        sq = x * x
        s_lo = jnp.sum(jnp.where(lo, sq, 0.0), axis=-1, keepdims=True)
        s_hi = jnp.sum(jnp.where(lo, 0.0, sq), axis=-1, keepdims=True)
        ms = jnp.where(lo, s_lo, s_hi) * (1.0 / DIFF_HALF)
        return (x * lax.rsqrt(ms + EPS)) * g

    def rope(y):
        sw = jnp.where(even, pltpu.roll(y, HEAD_DIM - 1, 1), pltpu.roll(y, 1, 1))
        return y * cos + sw * sin

    dq = dq_ref[...]
    dk = dk_ref[...]
    gq = gq_ref[...]
    gk = gk_ref[...]
    for h in range(DIFF_HEADS):
        c = h * HEAD_DIM
        sl = slice(c, c + HEAD_DIM)
        q = half_rms(p_ref[:, sl], dq) * (DIFF_HALF ** -0.5)
        qa = qaug * slopes[h] + tail
        aq1_ref[:, sl] = jnp.where(lo, q, qa).astype(BF16)
        aq2_ref[:, sl] = jnp.where(lo, pltpu.roll(q, DIFF_HALF, 1), qa).astype(BF16)
        k = half_rms(p_ref[:, DIFF_WIDTH + c:DIFF_WIDTH + c + HEAD_DIM], dk)
        ak1_ref[:, sl] = jnp.where(lo, k, kaug).astype(BF16)
        ak2_ref[:, sl] = jnp.where(lo, pltpu.roll(k, DIFF_HALF, 1), kaug).astype(BF16)
        av_ref[:, sl] = p_ref[:, 2 * DIFF_WIDTH + c:2 * DIFF_WIDTH + c + HEAD_DIM].astype(BF16)
    b0 = 3 * DIFF_WIDTH
    for h in range(GQA_HEADS):
        c = h * HEAD_DIM
        q = rope(_rms_rows(p_ref[:, b0 + c:b0 + c + HEAD_DIM], gq)) * (HEAD_DIM ** -0.5 * LOG2E)
        bq_ref[:, c:c + HEAD_DIM] = q.astype(BF16)
    k0 = b0 + GQA_WIDTH
    v0 = k0 + GQA_KV_WIDTH
    for h in range(GQA_KV_HEADS):
        c = h * HEAD_DIM
        k = rope(_rms_rows(p_ref[:, k0 + c:k0 + c + HEAD_DIM], gk))
        bk_ref[:, c:c + HEAD_DIM] = k.astype(BF16)
        bv_ref[:, c:c + HEAD_DIM] = p_ref[:, v0 + c:v0 + c + HEAD_DIM].astype(BF16)


def _prep(proj, tables, shift, dq, dk, gq, gk, *, seq, slopes, tm=512):
    n, w = proj.shape
    tm = min(tm, seq)
    nt = seq // tm
    row = lambda i: (i, 0)
    pos = lambda i: (i % nt, 0)
    fixed = lambda i: (0, 0)
    dq2 = jnp.concatenate([dq, dq]).reshape(1, HEAD_DIM)
    dk2 = jnp.concatenate([dk, dk]).reshape(1, HEAD_DIM)
    shapes = [(n, DIFF_WIDTH)] * 5 + [(n, GQA_WIDTH), (n, GQA_KV_WIDTH), (n, GQA_KV_WIDTH)]
    return pl.pallas_call(
        functools.partial(_prep_kernel, slopes=slopes),
        grid=(n // tm,),
        in_specs=[pl.BlockSpec(memory_space=pltpu.SMEM), pl.BlockSpec((tm, w), row)]
        + [pl.BlockSpec((tm, HEAD_DIM), pos)] * 4
        + [pl.BlockSpec((1, HEAD_DIM), fixed)] * 4,
        out_specs=[pl.BlockSpec((tm, s[1]), row) for s in shapes],
        out_shape=[jax.ShapeDtypeStruct(s, BF16) for s in shapes],
        compiler_params=_params("parallel"),
        name="head_prep",
    )(shift.reshape(1), proj, *tables, dq2, dk2, gq.reshape(1, HEAD_DIM), gk.reshape(1, HEAD_DIM))


def _position_tables(seq, tile):
    rows_n = seq // GRID_W
    row = jnp.repeat(jnp.arange(rows_n, dtype=F32), GRID_W, total_repeat_length=seq)
    col = jnp.tile(jnp.arange(GRID_W, dtype=F32), rows_n)
    inv = ROPE_THETA ** (-jnp.arange(0, ROPE_AXIS_DIM, 2, dtype=F32) / ROPE_AXIS_DIM)
    ang = jnp.concatenate([row[:, None] * inv, col[:, None] * inv], axis=-1)
    cos = jnp.repeat(jnp.cos(ang), 2, axis=-1)
    sin = jnp.sin(ang)
    sin_signed = jnp.stack([-sin, sin], axis=-1).reshape(seq, HEAD_DIM)
    loc = np.arange(seq) % tile
    hi = (loc // 16).astype(np.float32)
    lo = (loc % 16).astype(np.float32)
    qaug = np.zeros((seq, HEAD_DIM), np.float32)
    kaug = np.zeros((seq, HEAD_DIM), np.float32)
    qaug[:, AUG0 + 0] = 16.0
    qaug[:, AUG0 + 1] = 1.0
    qaug[:, AUG0 + 2] = -16.0 * hi
    qaug[:, AUG0 + 3] = -lo
    kaug[:, AUG0 + 0] = hi
    kaug[:, AUG0 + 1] = lo
    kaug[:, AUG0 + 2:AUG0 + 6] = 1.0
    return cos, sin_signed, jnp.asarray(qaug), jnp.asarray(kaug)


SHIFT_MAX = 40.0
UNDERFLOW = 106.0


def _qk(q, k):
    return lax.dot_general(q, k, (((1,), (1,)), ((), ())), preferred_element_type=F32)


def _lane_fold(p):
    out = p[:, :LANES]
    for c in range(1, p.shape[1] // LANES):
        out = out + p[:, c * LANES:(c + 1) * LANES]
    return out


def _online_update(s, v, m_ref, l_ref, acc_ref, idx, exp_fn):
    m_old = m_ref[idx]
    m_new = jnp.maximum(m_old, jnp.max(s, axis=-1, keepdims=True))
    alpha = exp_fn(m_old - m_new)
    p = exp_fn(s - m_new)
    l_ref[idx] = alpha * l_ref[idx] + jnp.sum(p, axis=-1, keepdims=True)
    acc_ref[idx] = alpha * acc_ref[idx] + jnp.dot(p.astype(BF16), v, preferred_element_type=F32)
    m_ref[idx] = m_new


def _shifted_update(p, v, l_ref, acc_ref, idx):
    l_ref[idx] += _lane_fold(p)
    acc_ref[idx] += jnp.dot(p.astype(BF16), v, preferred_element_type=F32)


def _row_total(l_ref, idx, stable):
    return l_ref[idx] if stable else jnp.sum(l_ref[idx], axis=-1, keepdims=True)


def _diff_attn_kernel(lam_ref, q1_ref, q2_ref, k1_ref, k2_ref, v_ref, g_ref, o_ref, *scratch,
                      slope, t, qsub, nk, nj, window, out_scale, stable):
    if stable:
        m_ref, l_ref, acc_ref = scratch
    else:
        l_ref, acc_ref = scratch
    qb = pl.program_id(1)
    j = pl.program_id(2)
    ki = (qsub * qb - window + j) if window is not None else j

    @pl.when(j == 0)
    def _():
        if stable:
            m_ref[...] = jnp.full(m_ref.shape, -jnp.inf, F32)
        l_ref[...] = jnp.zeros(l_ref.shape, F32)
        acc_ref[...] = jnp.zeros(acc_ref.shape, F32)

    def tile(sub, d, bias):
        rows = pl.ds(sub * t, t)
        lane = lax.broadcasted_iota(jnp.int32, (1, HEAD_DIM), 1)
        sig = jnp.where(d > 0, 1.0, jnp.where(d < 0, -1.0, 0.0))
        tval = (-slope * t) * jnp.abs(d).astype(F32)
        mult = jnp.where(lane < AUG0, 1.0,
                         jnp.where(lane < AUG0 + 4, sig, jnp.where(lane == AUG0 + 4, tval, 1.0))).astype(BF16)
        v = v_ref[...]
        for c, (q_ref, k_ref) in enumerate(((q1_ref, k1_ref), (q2_ref, k2_ref))):
            s = _qk(q_ref[rows, :] * mult, k_ref[...])
            if bias is not None:
                s = s + bias
            if stable:
                _online_update(s, v, m_ref, l_ref, acc_ref, qsub * c + sub, jnp.exp)
            else:
                _shifted_update(jnp.exp(s), v, l_ref, acc_ref, qsub * c + sub)

    in_range = (ki >= 0) & (ki < nk)
    offs = [qsub * qb + sub - ki for sub in range(qsub)]
    offdiag = [in_range & (d != 0) if window is None else in_range & (d != 0) & (jnp.abs(d) <= window)
               for d in offs]
    all_off = functools.reduce(jnp.logical_and, offdiag)

    @pl.when(all_off)
    def _():
        for sub, d in enumerate(offs):
            tile(sub, d, None)

    for sub, d in enumerate(offs):
        @pl.when(offdiag[sub] & jnp.logical_not(all_off))
        def _():
            tile(sub, d, None)

        @pl.when(d == 0)
        def _():
            r = lax.broadcasted_iota(jnp.int32, (t, t), 0)
            c = lax.broadcasted_iota(jnp.int32, (t, t), 1)
            tile(sub, d, jnp.abs(r - c).astype(F32) * (-slope))

    @pl.when(j == nj - 1)
    def _():
        for sub in range(qsub):
            o1 = acc_ref[sub] / _row_total(l_ref, sub, stable)
            o2 = acc_ref[qsub + sub] / _row_total(l_ref, qsub + sub, stable)
            o = _rms_rows(o1 - lam_ref[0] * o2, g_ref[...]) * out_scale
            o_ref[pl.ds(sub * t, t), :] = o.astype(o_ref.dtype)


def _diff_window(slope, tile, nk):
    w = int(math.floor((UNDERFLOW / slope - 1.0) / tile)) + 1
    return None if 2 * w + 1 >= nk else w


def _diff_attention_head(h, aq1, aq2, ak1, ak2, av, lam, subln, *, batch, seq, out_scale, stable):
    n = aq1.shape[0]
    t = min(ATTN_TILE, seq)
    nk = seq // t
    qsub = DIFF_QSUB if nk % DIFF_QSUB == 0 else 1
    nqb = nk // qsub
    slope = 2.0 ** (-8.0 * (h + 1) / DIFF_HEADS)
    window = None if stable else _diff_window(slope, t, nk)
    nj = nk if window is None else 2 * window + qsub
    qmap = lambda b, qb, j: (b * nqb + qb, h)
    if window is None:
        kmap = lambda b, qb, j: (b * nk + j, h)
    else:
        kmap = lambda b, qb, j: (b * nk + jnp.clip(qsub * qb - window + j, 0, nk - 1), h)
    kern = functools.partial(_diff_attn_kernel, slope=slope, t=t, qsub=qsub, nk=nk, nj=nj, window=window,
                             out_scale=out_scale, stable=stable)
    stat = [pltpu.VMEM((2 * qsub, t, 1), F32)] * 2 if stable else [pltpu.VMEM((2 * qsub, t, LANES), F32)]
    return pl.pallas_call(
        kern,
        grid=(batch, nqb, nj),
        in_specs=[
            pl.BlockSpec(memory_space=pltpu.SMEM),
            pl.BlockSpec((qsub * t, HEAD_DIM), qmap),
            pl.BlockSpec((qsub * t, HEAD_DIM), qmap),
            pl.BlockSpec((t, HEAD_DIM), kmap),
            pl.BlockSpec((t, HEAD_DIM), kmap),
            pl.BlockSpec((t, HEAD_DIM), kmap),
            pl.BlockSpec((1, HEAD_DIM), lambda b, qb, j: (0, 0)),
        ],
        out_specs=pl.BlockSpec((qsub * t, HEAD_DIM), lambda b, qb, j: (b * nqb + qb, 0)),
        out_shape=jax.ShapeDtypeStruct((n, HEAD_DIM), BF16),
        scratch_shapes=stat + [pltpu.VMEM((2 * qsub, t, HEAD_DIM), F32)],
        compiler_params=_params("parallel", "parallel", "arbitrary"),
        name=f"diff_attention_h{h}" + ("_stable" if stable else ""),
    )(lam.reshape(1), aq1, aq2, ak1, ak2, av, subln.reshape(1, HEAD_DIM))


def _diff_attention(aq1, aq2, ak1, ak2, av, lam, subln, bound_ok, *, batch, seq, out_scale):
    def run(stable):
        def f(aq1, aq2, ak1, ak2, av, lam, subln):
            heads = [_diff_attention_head(h, aq1, aq2, ak1, ak2, av, lam, subln, batch=batch, seq=seq,
                                          out_scale=out_scale, stable=stable) for h in range(DIFF_HEADS)]
            return jnp.concatenate(heads, axis=-1)
        return f
    return lax.cond(bound_ok, run(False), run(True), aq1, aq2, ak1, ak2, av, lam, subln)


def _gqa_attn_kernel(shift_ref, q_ref, k_ref, v_ref, o_ref, *scratch, nk, stable):
    if stable:
        m_ref, l_ref, acc_ref = scratch
    else:
        l_ref, acc_ref = scratch
    ki = pl.program_id(3)

    @pl.when(ki == 0)
    def _():
        if stable:
            m_ref[...] = jnp.full(m_ref.shape, -jnp.inf, F32)
        l_ref[...] = jnp.zeros(l_ref.shape, F32)
        acc_ref[...] = jnp.zeros(acc_ref.shape, F32)

    k = k_ref[...]
    v = v_ref[...]
    for r in range(GQA_GROUP):
        s = _qk(q_ref[:, r * HEAD_DIM:(r + 1) * HEAD_DIM], k)
        if stable:
            _online_update(s, v, m_ref, l_ref, acc_ref, r, jnp.exp2)
        else:
            _shifted_update(jnp.exp2(s - shift_ref[0]), v, l_ref, acc_ref, r)

    @pl.when(ki == nk - 1)
    def _():
        for r in range(GQA_GROUP):
            o = acc_ref[r] / _row_total(l_ref, r, stable)
            o_ref[:, r * HEAD_DIM:(r + 1) * HEAD_DIM] = o.astype(o_ref.dtype)


def _gqa_attention_call(shift, bq, bk, bv, *, batch, seq, stable):
    n = bq.shape[0]
    tk = min(GQA_KTILE, seq)
    tq = min(GQA_QTILE, seq)
    nq, nk = seq // tq, seq // tk
    gw = GQA_GROUP * HEAD_DIM
    qmap = lambda b, g, qi, ki: (b * nq + qi, g)
    kmap = lambda b, g, qi, ki: (b * nk + ki, g)
    stat = [pltpu.VMEM((GQA_GROUP, tq, 1), F32)] * 2 if stable else [pltpu.VMEM((GQA_GROUP, tq, LANES), F32)]
    return pl.pallas_call(
        functools.partial(_gqa_attn_kernel, nk=nk, stable=stable),
        grid=(batch, GQA_KV_HEADS, nq, nk),
        in_specs=[
            pl.BlockSpec(memory_space=pltpu.SMEM),
            pl.BlockSpec((tq, gw), qmap),
            pl.BlockSpec((tk, HEAD_DIM), kmap),
            pl.BlockSpec((tk, HEAD_DIM), kmap),
        ],
        out_specs=pl.BlockSpec((tq, gw), qmap),
        out_shape=jax.ShapeDtypeStruct((n, GQA_WIDTH), BF16),
        scratch_shapes=stat + [pltpu.VMEM((GQA_GROUP, tq, HEAD_DIM), F32)],
        compiler_params=_params("parallel", "parallel", "parallel", "arbitrary"),
        name="gqa_attention" + ("_stable" if stable else ""),
    )(shift.reshape(1), bq, bk, bv)


def _gqa_attention(bq, bk, bv, bound, *, batch, seq):
    shift = bound * LOG2E
    return lax.cond(
        bound <= SHIFT_MAX,
        functools.partial(_gqa_attention_call, batch=batch, seq=seq, stable=False),
        functools.partial(_gqa_attention_call, batch=batch, seq=seq, stable=True),
        shift, bq, bk, bv)


def _out_proj_kernel(a_ref, b_ref, wa_ref, wb_ref, x_ref, o_ref):
    acc = jnp.dot(a_ref[...], wa_ref[...], preferred_element_type=F32)
    acc = acc + jnp.dot(b_ref[...], wb_ref[...], preferred_element_type=F32)
    o_ref[...] = x_ref[...] + acc


def _out_proj(ao, bo, w_out, x, *, tm=1024, tn=512):
    n, d = x.shape
    tm = min(tm, n)
    tn = min(tn, d)
    wa, wb = ao.shape[1], bo.shape[1]
    return pl.pallas_call(
        _out_proj_kernel,
        grid=(n // tm, d // tn),
        in_specs=[
            pl.BlockSpec((tm, wa), lambda i, j: (i, 0)),
            pl.BlockSpec((tm, wb), lambda i, j: (i, 0)),
            pl.BlockSpec((wa, tn), lambda i, j: (0, j)),
            pl.BlockSpec((wb, tn), lambda i, j: (wa // wb, j)),
            pl.BlockSpec((tm, tn), lambda i, j: (i, j)),
        ],
        out_specs=pl.BlockSpec((tm, tn), lambda i, j: (i, j)),
        out_shape=jax.ShapeDtypeStruct((n, d), F32),
        compiler_params=_params("parallel", "parallel"),
        name="out_proj",
    )(ao, bo, w_out, w_out, x)


def _swiglu_act(g, u):
    return (g * jax.nn.sigmoid(g)) * u


def _ffn_kernel(x_ref, gn_ref, wg_ref, wu_ref, wd_ref, o_ref, h_ref):
    @pl.when(pl.program_id(1) == 0)
    def _():
        xf = x_ref[...]
        h_ref[...] = _rms_rows(xf, gn_ref[...]).astype(h_ref.dtype)
        o_ref[...] = xf

    h = h_ref[...]
    g = jnp.dot(h, wg_ref[...], preferred_element_type=F32)
    u = jnp.dot(h, wu_ref[...], preferred_element_type=F32)
    a = _swiglu_act(g, u).astype(BF16)
    o_ref[...] += jnp.dot(a, wd_ref[...], preferred_element_type=F32)


def _ffn_dense(x, gn, wg, wu, wd, *, tm=512, tf=512):
    n, d = x.shape
    f = wg.shape[1]
    tm = min(tm, n)
    return pl.pallas_call(
        _ffn_kernel,
        grid=(n // tm, f // tf),
        in_specs=[
            pl.BlockSpec((tm, d), lambda i, j: (i, 0)),
            pl.BlockSpec((1, d), lambda i, j: (0, 0)),
            pl.BlockSpec((d, tf), lambda i, j: (0, j)),
            pl.BlockSpec((d, tf), lambda i, j: (0, j)),
            pl.BlockSpec((tf, d), lambda i, j: (j, 0)),
        ],
        out_specs=pl.BlockSpec((tm, d), lambda i, j: (i, 0)),
        out_shape=jax.ShapeDtypeStruct((n, d), F32),
        scratch_shapes=[pltpu.VMEM((tm, d), BF16)],
        compiler_params=_params("parallel", "arbitrary"),
        name="ffn_dense",
    )(x, gn.reshape(1, d), wg, wu, wd)


def _pad_ff(w, axis, mult):
    f = w.shape[axis]
    fp = -(-f // mult) * mult
    if fp == f:
        return w
    pad = [(0, 0)] * w.ndim
    pad[axis] = (0, fp - f)
    return jnp.pad(w, pad)


def _router_kernel(x_ref, gn_ref, r_ref, h_ref, idx_ref, gate_ref):
    h = _rms_rows(x_ref[...], gn_ref[...])
    h_ref[...] = h
    logits = jnp.dot(h.astype(BF16), r_ref[...], preferred_element_type=F32)
    lane = lax.broadcasted_iota(jnp.int32, logits.shape, 1)
    m1 = jnp.max(logits, axis=-1, keepdims=True)
    i1 = jnp.min(jnp.where(logits == m1, lane, N_EXPERTS), axis=-1, keepdims=True)
    rest = jnp.where(lane == i1, -jnp.inf, logits)
    m2 = jnp.max(rest, axis=-1, keepdims=True)
    i2 = jnp.min(jnp.where(rest == m2, lane, N_EXPERTS), axis=-1, keepdims=True)
    e2 = jnp.exp(m2 - m1)
    den = 1.0 + e2
    first = lax.broadcasted_iota(jnp.int32, idx_ref.shape, 1) == 0
    idx_ref[...] = jnp.where(first, i1, i2)
    gate_ref[...] = jnp.where(first, 1.0 / den, e2 / den)


def _router(x, gn, router, *, tm=512):
    n, d = x.shape
    tm = min(tm, n)
    return pl.pallas_call(
        _router_kernel,
        grid=(n // tm,),
        in_specs=[
            pl.BlockSpec((tm, d), lambda i: (i, 0)),
            pl.BlockSpec((1, d), lambda i: (0, 0)),
            pl.BlockSpec((d, N_EXPERTS), lambda i: (0, 0)),
        ],
        out_specs=[
            pl.BlockSpec((tm, d), lambda i: (i, 0)),
            pl.BlockSpec((tm, TOP_K), lambda i: (i, 0)),
            pl.BlockSpec((tm, TOP_K), lambda i: (i, 0)),
        ],
        out_shape=[
            jax.ShapeDtypeStruct((n, d), F32),
            jax.ShapeDtypeStruct((n, TOP_K), jnp.int32),
            jax.ShapeDtypeStruct((n, TOP_K), F32),
        ],
        compiler_params=_params("parallel"),
        name="moe_router",
    )(x, gn.reshape(1, d), router)


def _gather_rows(idx_ref, src_ref, dst_ref, sem, rows):
    def start(r, carry):
        pltpu.make_async_copy(src_ref.at[pl.ds(idx_ref[r], 1)], dst_ref.at[pl.ds(r, 1)], sem).start()
        return carry

    lax.fori_loop(0, rows, start, 0, unroll=8)
    pltpu.make_async_copy(src_ref.at[pl.ds(0, rows)], dst_ref.at[pl.ds(0, rows)], sem).wait()


def _gather_kernel(idx_ref, src_ref, o_ref, sem, *, rows):
    _gather_rows(idx_ref, src_ref, o_ref, sem, rows)


def _gather(src, idx, *, rows=512):
    r = idx.shape[0]
    d = src.shape[1]
    rows = min(rows, r)
    return pl.pallas_call(
        functools.partial(_gather_kernel, rows=rows),
        grid=(r // rows,),
        in_specs=[
            pl.BlockSpec((rows,), lambda i: (i,), memory_space=pltpu.SMEM),
            pl.BlockSpec(memory_space=pl.ANY),
        ],
        out_specs=pl.BlockSpec((rows, d), lambda i: (i, 0)),
        out_shape=jax.ShapeDtypeStruct((r, d), src.dtype),
        scratch_shapes=[pltpu.SemaphoreType.DMA],
        compiler_params=_params("arbitrary"),
        name="moe_gather",
    )(idx, src)


def _moe_ffn_kernel(te_ref, nu_ref, h_ref, gate_ref, wg_ref, wu_ref, wd_ref, o_ref, hb_ref):
    i = pl.program_id(0)
    j = pl.program_id(1)
    used = i < nu_ref[0]

    @pl.when(j == 0)
    def _():
        hb_ref[...] = h_ref[...].astype(BF16)
        o_ref[...] = jnp.zeros(o_ref.shape, F32)

    @pl.when(used)
    def _():
        h = hb_ref[...]
        g = jnp.dot(h, wg_ref[...], preferred_element_type=F32)
        u = jnp.dot(h, wu_ref[...], preferred_element_type=F32)
        a = (_swiglu_act(g, u) * gate_ref[...]).astype(BF16)
        o_ref[...] += jnp.dot(a, wd_ref[...], preferred_element_type=F32)


def _moe_ffn(tile_expert, n_used, h_sorted, gate_sorted, wg, wu, wd, *, tm, tf=512):
    r, d = h_sorted.shape
    f = wg.shape[2]
    nf = f // tf
    jj = lambda i, j, te, nu: jnp.where(i < nu[0], j, nf - 1)
    return pl.pallas_call(
        _moe_ffn_kernel,
        grid_spec=pltpu.PrefetchScalarGridSpec(
            num_scalar_prefetch=2,
            grid=(r // tm, nf),
            in_specs=[
                pl.BlockSpec((tm, d), lambda i, j, te, nu: (i, 0)),
                pl.BlockSpec((tm, 1), lambda i, j, te, nu: (i, 0)),
                pl.BlockSpec((None, d, tf), lambda i, j, te, nu: (te[i], 0, jj(i, j, te, nu))),
                pl.BlockSpec((None, d, tf), lambda i, j, te, nu: (te[i], 0, jj(i, j, te, nu))),
                pl.BlockSpec((None, tf, d), lambda i, j, te, nu: (te[i], jj(i, j, te, nu), 0)),
            ],
            out_specs=pl.BlockSpec((tm, d), lambda i, j, te, nu: (i, 0)),
            scratch_shapes=[pltpu.VMEM((tm, d), BF16)],
        ),
        out_shape=jax.ShapeDtypeStruct((r, d), F32),
        compiler_params=_params("arbitrary", "arbitrary"),
        name="moe_experts",
    )(tile_expert, n_used, h_sorted, gate_sorted, wg, wu, wd)


def _combine_kernel(idx_ref, y_ref, x_ref, o_ref, buf_ref, sem, *, tc):
    _gather_rows(idx_ref, y_ref, buf_ref, sem, TOP_K * tc)
    o_ref[...] = x_ref[...] + buf_ref[pl.ds(0, tc), :] + buf_ref[pl.ds(tc, tc), :]


def _combine(y_sorted, pos, x, *, tc=256):
    n, d = x.shape
    tc = min(tc, n)
    idx = pos.reshape(n // tc, tc, TOP_K).transpose(0, 2, 1).reshape(-1)
    return pl.pallas_call(
        functools.partial(_combine_kernel, tc=tc),
        grid=(n // tc,),
        in_specs=[
            pl.BlockSpec((TOP_K * tc,), lambda i: (i,), memory_space=pltpu.SMEM),
            pl.BlockSpec(memory_space=pl.ANY),
            pl.BlockSpec((tc, d), lambda i: (i, 0)),
        ],
        out_specs=pl.BlockSpec((tc, d), lambda i: (i, 0)),
        out_shape=jax.ShapeDtypeStruct((n, d), F32),
        scratch_shapes=[pltpu.VMEM((TOP_K * tc, d), F32), pltpu.SemaphoreType.DMA],
        compiler_params=_params("arbitrary"),
        name="moe_combine",
    )(idx, y_sorted, x)


def _route_plan(idx, gate, tm):
    n = idx.shape[0]
    a = n * TOP_K
    rows = a + N_EXPERTS * tm
    e_flat = idx.reshape(a)
    order = jnp.argsort(e_flat, stable=True).astype(jnp.int32)
    counts = jnp.sum(e_flat[:, None] == jnp.arange(N_EXPERTS)[None, :], axis=0).astype(jnp.int32)
    padded = ((counts + tm - 1) // tm) * tm
    pad_end = jnp.cumsum(padded)
    pad_start = pad_end - padded
    raw_start = jnp.cumsum(counts) - counts
    r = jnp.arange(rows, dtype=jnp.int32)
    tile_expert = jnp.minimum(jnp.sum(r[::tm, None] >= pad_end[None, :], axis=1), N_EXPERTS - 1).astype(jnp.int32)
    row_expert = jnp.repeat(tile_expert, tm, total_repeat_length=rows)
    rank = r - pad_start[row_expert]
    valid = rank < counts[row_expert]
    row_assign = order[jnp.clip(raw_start[row_expert] + rank, 0, a - 1)]
    row_token = jnp.where(valid, row_assign // TOP_K, 0)
    row_gate = jnp.where(valid, gate.reshape(a)[row_assign], 0.0)
    sorted_pos = jnp.argsort(order).astype(jnp.int32)
    pos = pad_start[e_flat] + sorted_pos - raw_start[e_flat]
    n_used = (pad_end[-1] // tm).reshape(1).astype(jnp.int32)
    return row_token, row_gate.reshape(rows, 1), pos.reshape(n, TOP_K), tile_expert, n_used


def _moe(x, gn, router, wg, wu, wd, *, tm=512):
    n = x.shape[0]
    tm = min(tm, n)
    h, idx, gate = _router(x, gn, router)
    row_token, row_gate, pos, tile_expert, n_used = _route_plan(idx, gate, tm)
    h_sorted = _gather(h, row_token, rows=tm)
    y_sorted = _moe_ffn(tile_expert, n_used, h_sorted, row_gate, wg, wu, wd, tm=tm)
    return _combine(y_sorted, pos, x)


def kernel(x, attn_norm, w_in, diff_q_norm, diff_k_norm, diff_lambda, diff_subln, gqa_q_norm, gqa_k_norm, w_out, ffn_norm, dense_w_gate, dense_w_up, dense_w_down, moe_router, moe_w_gate, moe_w_up, moe_w_down):
    batch, seq, d = x.shape
    depth = w_in.shape[0]
    n = batch * seq
    xs = x.reshape(n, d).astype(F32)
    tables = _position_tables(seq, min(ATTN_TILE, seq))
    slopes = tuple(2.0 ** (-8.0 * (h + 1) / DIFF_HEADS) for h in range(DIFF_HEADS))
    ff_tile = 512
    bound_margin = 1.02

    for l in range(depth):
        proj = _norm_matmul(xs, attn_norm[l], w_in[l].astype(BF16))
        amax = lambda g: jnp.max(jnp.abs(g.astype(F32)))
        diff_bound = amax(diff_q_norm[l]) * amax(diff_k_norm[l]) * (DIFF_HALF ** 0.5 * bound_margin)
        gqa_bound = amax(gqa_q_norm[l]) * amax(gqa_k_norm[l]) * (HEAD_DIM ** 0.5 * bound_margin)
        diff_ok = diff_bound <= SHIFT_MAX
        diff_shift = jnp.where(diff_ok, diff_bound, 0.0)
        aq1, aq2, ak1, ak2, av, bq, bk, bv = _prep(
            proj, tables, diff_shift, diff_q_norm[l], diff_k_norm[l], gqa_q_norm[l], gqa_k_norm[l],
            seq=seq, slopes=slopes)
        lam_init = 0.8 - 0.6 * math.exp(-0.3 * l)
        lp = diff_lambda[l].astype(F32)
        lam = jnp.exp(jnp.sum(lp[0] * lp[1])) - jnp.exp(jnp.sum(lp[2] * lp[3])) + lam_init
        ao = _diff_attention(aq1, aq2, ak1, ak2, av, lam, diff_subln[l], diff_ok,
                             batch=batch, seq=seq, out_scale=1.0 - lam_init)
        bo = _gqa_attention(bq, bk, bv, gqa_bound, batch=batch, seq=seq)
        xs = _out_proj(ao, bo, w_out[l].astype(BF16), xs)

        i = l // 2
        if l % 2 == 0:
            wg = _pad_ff(dense_w_gate[i].astype(BF16), 1, ff_tile)
            wu = _pad_ff(dense_w_up[i].astype(BF16), 1, ff_tile)
            wd = _pad_ff(dense_w_down[i].astype(BF16), 0, ff_tile)
            xs = _ffn_dense(xs, ffn_norm[l], wg, wu, wd, tf=ff_tile)
        else:
            xs = _moe(xs, ffn_norm[l], moe_router[i].astype(BF16), moe_w_gate[i].astype(BF16),
                      moe_w_up[i].astype(BF16), moe_w_down[i].astype(BF16))
    return xs.reshape(batch, seq, d)
```

```python
import functools
import math

import numpy as np
import jax
import jax.numpy as jnp
from jax import lax
from jax.experimental import pallas as pl
from jax.experimental.pallas import tpu as pltpu

HEAD_DIM = 128
DIFF_HALF = HEAD_DIM // 2
DIFF_HEADS = 8
GQA_HEADS = 8
GQA_KV_HEADS = 2
GQA_GROUP = GQA_HEADS // GQA_KV_HEADS
DIFF_WIDTH = DIFF_HEADS * HEAD_DIM
GQA_WIDTH = GQA_HEADS * HEAD_DIM
GQA_KV_WIDTH = GQA_KV_HEADS * HEAD_DIM
GRID_W = 64
ROPE_THETA = 10000.0
ROPE_AXIS_DIM = HEAD_DIM // 2
N_EXPERTS = 8
TOP_K = 2
EPS = 1e-6

LANES = 128
VMEM_LIMIT_BYTES = 52 * 1024 * 1024

BF16 = jnp.bfloat16
F32 = jnp.float32


def _params(*sem):
    return pltpu.CompilerParams(dimension_semantics=sem, vmem_limit_bytes=VMEM_LIMIT_BYTES)


def _rms_rows(xf, g):
    ms = jnp.mean(xf * xf, axis=-1, keepdims=True)
    return (xf * lax.rsqrt(ms + EPS)) * g


def _norm_matmul_kernel(x_ref, g_ref, w_ref, o_ref, n_ref):
    @pl.when(pl.program_id(1) == 0)
    def _():
        n_ref[...] = _rms_rows(x_ref[...], g_ref[...]).astype(n_ref.dtype)

    o_ref[...] = jnp.dot(n_ref[...], w_ref[...], preferred_element_type=F32).astype(o_ref.dtype)


def _norm_matmul(x, g, w, *, tm=1024, tn=512, out_dtype=F32):
    n, d = x.shape
    wd = w.shape[1]
    tm = min(tm, n)
    tn = min(tn, wd)
    return pl.pallas_call(
        _norm_matmul_kernel,
        grid=(n // tm, wd // tn),
        in_specs=[
            pl.BlockSpec((tm, d), lambda i, j: (i, 0)),
            pl.BlockSpec((1, d), lambda i, j: (0, 0)),
            pl.BlockSpec((d, tn), lambda i, j: (0, j)),
        ],
        out_specs=pl.BlockSpec((tm, tn), lambda i, j: (i, j)),
        out_shape=jax.ShapeDtypeStruct((n, wd), out_dtype),
        scratch_shapes=[pltpu.VMEM((tm, d), BF16)],
        compiler_params=_params("parallel", "arbitrary"),
        name="norm_in_proj",
    )(x, g.reshape(1, d), w)


ATTN_TILE = 512
DIFF_QSUB = 4
GQA_QTILE = 1024
GQA_KTILE = 1024
AUG0 = DIFF_HALF
LOG2E = 1.4426950408889634


def _prep_kernel(shift_ref, p_ref, cos_ref, sin_ref, qaug_ref, kaug_ref, dq_ref, dk_ref, gq_ref, gk_ref,
                 aq1_ref, aq2_ref, ak1_ref, ak2_ref, av_ref, bq_ref, bk_ref, bv_ref, *, slopes):
    lane = lax.broadcasted_iota(jnp.int32, (1, HEAD_DIM), 1)
    lo = lane < DIFF_HALF
    even = (lane & 1) == 0
    cos = cos_ref[...]
    sin = sin_ref[...]
    qaug = qaug_ref[...]
    kaug = kaug_ref[...]
    tail = jnp.where(lane == AUG0 + 4, 1.0, 0.0) + jnp.where(lane == AUG0 + 5, -shift_ref[0], 0.0)

    def half_rms(x, g):
        sq = x * x
        s_lo = jnp.sum(jnp.where(lo, sq, 0.0), axis=-1, keepdims=True)
        s_hi = jnp.sum(jnp.where(lo, 0.0, sq), axis=-1, keepdims=True)
        ms = jnp.where(lo, s_lo, s_hi) * (1.0 / DIFF_HALF)
        return (x * lax.rsqrt(ms + EPS)) * g

    def rope(y):
        sw = jnp.where(even, pltpu.roll(y, HEAD_DIM - 1, 1), pltpu.roll(y, 1, 1))
        return y * cos + sw * sin

    dq = dq_ref[...]
    dk = dk_ref[...]
    gq = gq_ref[...]
    gk = gk_ref[...]
    for h in range(DIFF_HEADS):
        c = h * HEAD_DIM
        sl = slice(c, c + HEAD_DIM)
        q = half_rms(p_ref[:, sl], dq) * (DIFF_HALF ** -0.5)
        qa = qaug * slopes[h] + tail
        aq1_ref[:, sl] = jnp.where(lo, q, qa).astype(BF16)
        aq2_ref[:, sl] = jnp.where(lo, pltpu.roll(q, DIFF_HALF, 1), qa).astype(BF16)
        k = half_rms(p_ref[:, DIFF_WIDTH + c:DIFF_WIDTH + c + HEAD_DIM], dk)
        ak1_ref[:, sl] = jnp.where(lo, k, kaug).astype(BF16)
        ak2_ref[:, sl] = jnp.where(lo, pltpu.roll(k, DIFF_HALF, 1), kaug).astype(BF16)
        av_ref[:, sl] = p_ref[:, 2 * DIFF_WIDTH + c:2 * DIFF_WIDTH + c + HEAD_DIM].astype(BF16)
    b0 = 3 * DIFF_WIDTH
    for h in range(GQA_HEADS):
        c = h * HEAD_DIM
        q = rope(_rms_rows(p_ref[:, b0 + c:b0 + c + HEAD_DIM], gq)) * (HEAD_DIM ** -0.5 * LOG2E)
        bq_ref[:, c:c + HEAD_DIM] = q.astype(BF16)
    k0 = b0 + GQA_WIDTH
    v0 = k0 + GQA_KV_WIDTH
    for h in range(GQA_KV_HEADS):
        c = h * HEAD_DIM
        k = rope(_rms_rows(p_ref[:, k0 + c:k0 + c + HEAD_DIM], gk))
        bk_ref[:, c:c + HEAD_DIM] = k.astype(BF16)
        bv_ref[:, c:c + HEAD_DIM] = p_ref[:, v0 + c:v0 + c + HEAD_DIM].astype(BF16)


def _prep(proj, tables, shift, dq, dk, gq, gk, *, seq, slopes, tm=512):
    n, w = proj.shape
    tm = min(tm, seq)
    nt = seq // tm
    row = lambda i: (i, 0)
    pos = lambda i: (i % nt, 0)
    fixed = lambda i: (0, 0)
    dq2 = jnp.concatenate([dq, dq]).reshape(1, HEAD_DIM)
    dk2 = jnp.concatenate([dk, dk]).reshape(1, HEAD_DIM)
    shapes = [(n, DIFF_WIDTH)] * 5 + [(n, GQA_WIDTH), (n, GQA_KV_WIDTH), (n, GQA_KV_WIDTH)]
    return pl.pallas_call(
        functools.partial(_prep_kernel, slopes=slopes),
        grid=(n // tm,),
        in_specs=[pl.BlockSpec(memory_space=pltpu.SMEM), pl.BlockSpec((tm, w), row)]
        + [pl.BlockSpec((tm, HEAD_DIM), pos)] * 4
        + [pl.BlockSpec((1, HEAD_DIM), fixed)] * 4,
        out_specs=[pl.BlockSpec((tm, s[1]), row) for s in shapes],
        out_shape=[jax.ShapeDtypeStruct(s, BF16) for s in shapes],
        compiler_params=_params("parallel"),
        name="head_prep",
    )(shift.reshape(1), proj, *tables, dq2, dk2, gq.reshape(1, HEAD_DIM), gk.reshape(1, HEAD_DIM))


def _position_tables(seq, tile):
    rows_n = seq // GRID_W
    row = jnp.repeat(jnp.arange(rows_n, dtype=F32), GRID_W, total_repeat_length=seq)
    col = jnp.tile(jnp.arange(GRID_W, dtype=F32), rows_n)
    inv = ROPE_THETA ** (-jnp.arange(0, ROPE_AXIS_DIM, 2, dtype=F32) / ROPE_AXIS_DIM)
    ang = jnp.concatenate([row[:, None] * inv, col[:, None] * inv], axis=-1)
    cos = jnp.repeat(jnp.cos(ang), 2, axis=-1)
    sin = jnp.sin(ang)
    sin_signed = jnp.stack([-sin, sin], axis=-1).reshape(seq, HEAD_DIM)
    loc = np.arange(seq) % tile
    hi = (loc // 16).astype(np.float32)
    lo = (loc % 16).astype(np.float32)
    qaug = np.zeros((seq, HEAD_DIM), np.float32)
    kaug = np.zeros((seq, HEAD_DIM), np.float32)
    qaug[:, AUG0 + 0] = 16.0
    qaug[:, AUG0 + 1] = 1.0
    qaug[:, AUG0 + 2] = -16.0 * hi
    qaug[:, AUG0 + 3] = -lo
    kaug[:, AUG0 + 0] = hi
    kaug[:, AUG0 + 1] = lo
    kaug[:, AUG0 + 2:AUG0 + 6] = 1.0
    return cos, sin_signed, jnp.asarray(qaug), jnp.asarray(kaug)


SHIFT_MAX = 40.0
UNDERFLOW = 106.0


def _qk(q, k):
    return lax.dot_general(q, k, (((1,), (1,)), ((), ())), preferred_element_type=F32)


def _lane_fold(p):
    out = p[:, :LANES]
    for c in range(1, p.shape[1] // LANES):
        out = out + p[:, c * LANES:(c + 1) * LANES]
    return out


def _online_update(s, v, m_ref, l_ref, acc_ref, idx, exp_fn):
    m_old = m_ref[idx]
    m_new = jnp.maximum(m_old, jnp.max(s, axis=-1, keepdims=True))
    alpha = exp_fn(m_old - m_new)
    p = exp_fn(s - m_new)
    l_ref[idx] = alpha * l_ref[idx] + jnp.sum(p, axis=-1, keepdims=True)
    acc_ref[idx] = alpha * acc_ref[idx] + jnp.dot(p.astype(BF16), v, preferred_element_type=F32)
    m_ref[idx] = m_new


def _shifted_update(p, v, l_ref, acc_ref, idx):
    l_ref[idx] += _lane_fold(p)
    acc_ref[idx] += jnp.dot(p.astype(BF16), v, preferred_element_type=F32)


def _row_total(l_ref, idx, stable):
    return l_ref[idx] if stable else jnp.sum(l_ref[idx], axis=-1, keepdims=True)


def _diff_attn_kernel(lam_ref, q1_ref, q2_ref, k1_ref, k2_ref, v_ref, g_ref, o_ref, *scratch,
                      slope, t, qsub, nk, window, out_scale, stable):
    if stable:
        m_ref, l_ref, acc_ref = scratch
        m_ref[...] = jnp.full(m_ref.shape, -jnp.inf, F32)
    else:
        l_ref, acc_ref = scratch
    l_ref[...] = jnp.zeros(l_ref.shape, F32)
    acc_ref[...] = jnp.zeros(acc_ref.shape, F32)
    q0 = qsub * pl.program_id(1)

    def tile(sub, d, ki, bias):
        rows = pl.ds(sub * t, t)
        keys = pl.ds(pl.multiple_of(ki * t, t), t)
        lane = lax.broadcasted_iota(jnp.int32, (1, HEAD_DIM), 1)
        sig = jnp.where(d > 0, 1.0, jnp.where(d < 0, -1.0, 0.0))
        tval = (-slope * t) * jnp.abs(d).astype(F32)
        mult = jnp.where(lane < AUG0, 1.0,
                         jnp.where(lane < AUG0 + 4, sig, jnp.where(lane == AUG0 + 4, tval, 1.0))).astype(BF16)
        v = v_ref[keys, :]
        for c, (q_ref, k_ref) in enumerate(((q1_ref, k1_ref), (q2_ref, k2_ref))):
            s = _qk(q_ref[rows, :] * mult, k_ref[keys, :])
            if bias is not None:
                s = s + bias
            if stable:
                _online_update(s, v, m_ref, l_ref, acc_ref, qsub * c + sub, jnp.exp)
            else:
                _shifted_update(jnp.exp(s), v, l_ref, acc_ref, qsub * c + sub)

    def key_tile(ki, carry):
        offs = [q0 + sub - ki for sub in range(qsub)]
        offdiag = [(d != 0) if window is None else (d != 0) & (jnp.abs(d) <= window) for d in offs]
        all_off = functools.reduce(jnp.logical_and, offdiag)

        @pl.when(all_off)
        def _():
            for sub, d in enumerate(offs):
                tile(sub, d, ki, None)

        for sub, d in enumerate(offs):
            @pl.when(offdiag[sub] & jnp.logical_not(all_off))
            def _():
                tile(sub, d, ki, None)

            @pl.when(d == 0)
            def _():
                r = lax.broadcasted_iota(jnp.int32, (t, t), 0)
                c = lax.broadcasted_iota(jnp.int32, (t, t), 1)
                tile(sub, d, ki, jnp.abs(r - c).astype(F32) * (-slope))
        return carry

    if window is None:
        lo, hi = 0, nk
    else:
        lo = jnp.maximum(q0 - window, 0)
        hi = jnp.minimum(q0 + qsub + window, nk)
    lax.fori_loop(lo, hi, key_tile, 0)

    for sub in range(qsub):
        o1 = acc_ref[sub] / _row_total(l_ref, sub, stable)
        o2 = acc_ref[qsub + sub] / _row_total(l_ref, qsub + sub, stable)
        o = _rms_rows(o1 - lam_ref[0] * o2, g_ref[...]) * out_scale
        o_ref[pl.ds(sub * t, t), :] = o.astype(o_ref.dtype)


def _diff_window(slope, tile, nk):
    w = int(math.floor((UNDERFLOW / slope - 1.0) / tile)) + 1
    return None if 2 * w + 1 >= nk else w


def _diff_attention_head(h, aq1, aq2, ak1, ak2, av, lam, subln, *, batch, seq, out_scale, stable):
    n = aq1.shape[0]
    t = min(ATTN_TILE, seq)
    nk = seq // t
    qsub = DIFF_QSUB if nk % DIFF_QSUB == 0 else 1
    nqb = nk // qsub
    slope = 2.0 ** (-8.0 * (h + 1) / DIFF_HEADS)
    window = None if stable else _diff_window(slope, t, nk)
    qmap = lambda b, qb: (b * nqb + qb, h)
    kmap = lambda b, qb: (b, h)
    kern = functools.partial(_diff_attn_kernel, slope=slope, t=t, qsub=qsub, nk=nk, window=window,
                             out_scale=out_scale, stable=stable)
    stat = [pltpu.VMEM((2 * qsub, t, 1), F32)] * 2 if stable else [pltpu.VMEM((2 * qsub, t, LANES), F32)]
    return pl.pallas_call(
        kern,
        grid=(batch, nqb),
        in_specs=[
            pl.BlockSpec(memory_space=pltpu.SMEM),
            pl.BlockSpec((qsub * t, HEAD_DIM), qmap),
            pl.BlockSpec((qsub * t, HEAD_DIM), qmap),
            pl.BlockSpec((seq, HEAD_DIM), kmap),
            pl.BlockSpec((seq, HEAD_DIM), kmap),
            pl.BlockSpec((seq, HEAD_DIM), kmap),
            pl.BlockSpec((1, HEAD_DIM), lambda b, qb: (0, 0)),
        ],
        out_specs=pl.BlockSpec((qsub * t, HEAD_DIM), lambda b, qb: (b * nqb + qb, 0)),
        out_shape=jax.ShapeDtypeStruct((n, HEAD_DIM), BF16),
        scratch_shapes=stat + [pltpu.VMEM((2 * qsub, t, HEAD_DIM), F32)],
        compiler_params=_params("parallel", "parallel"),
        name=f"diff_attention_h{h}" + ("_stable" if stable else ""),
    )(lam.reshape(1), aq1, aq2, ak1, ak2, av, subln.reshape(1, HEAD_DIM))


def _diff_attention(aq1, aq2, ak1, ak2, av, lam, subln, bound_ok, *, batch, seq, out_scale):
    def run(stable):
        def f(aq1, aq2, ak1, ak2, av, lam, subln):
            heads = [_diff_attention_head(h, aq1, aq2, ak1, ak2, av, lam, subln, batch=batch, seq=seq,
                                          out_scale=out_scale, stable=stable) for h in range(DIFF_HEADS)]
            return jnp.concatenate(heads, axis=-1)
        return f
    return lax.cond(bound_ok, run(False), run(True), aq1, aq2, ak1, ak2, av, lam, subln)


def _gqa_attn_kernel(shift_ref, q_ref, k_ref, v_ref, o_ref, *scratch, tk, nk, stable):
    if stable:
        m_ref, l_ref, acc_ref = scratch
        m_ref[...] = jnp.full(m_ref.shape, -jnp.inf, F32)
    else:
        l_ref, acc_ref = scratch
    l_ref[...] = jnp.zeros(l_ref.shape, F32)
    acc_ref[...] = jnp.zeros(acc_ref.shape, F32)

    def key_tile(ki, carry):
        keys = pl.ds(pl.multiple_of(ki * tk, tk), tk)
        k = k_ref[keys, :]
        v = v_ref[keys, :]
        for r in range(GQA_GROUP):
            s = _qk(q_ref[:, r * HEAD_DIM:(r + 1) * HEAD_DIM], k)
            if stable:
                _online_update(s, v, m_ref, l_ref, acc_ref, r, jnp.exp2)
            else:
                _shifted_update(jnp.exp2(s - shift_ref[0]), v, l_ref, acc_ref, r)
        return carry

    lax.fori_loop(0, nk, key_tile, 0)

    for r in range(GQA_GROUP):
        o = acc_ref[r] / _row_total(l_ref, r, stable)
        o_ref[:, r * HEAD_DIM:(r + 1) * HEAD_DIM] = o.astype(o_ref.dtype)


def _gqa_attention_call(shift, bq, bk, bv, *, batch, seq, stable):
    n = bq.shape[0]
    tk = min(GQA_KTILE, seq)
    tq = min(GQA_QTILE, seq)
    nq, nk = seq // tq, seq // tk
    gw = GQA_GROUP * HEAD_DIM
    qmap = lambda b, g, qi: (b * nq + qi, g)
    kmap = lambda b, g, qi: (b, g)
    stat = [pltpu.VMEM((GQA_GROUP, tq, 1), F32)] * 2 if stable else [pltpu.VMEM((GQA_GROUP, tq, LANES), F32)]
    return pl.pallas_call(
        functools.partial(_gqa_attn_kernel, tk=tk, nk=nk, stable=stable),
        grid=(batch, GQA_KV_HEADS, nq),
        in_specs=[
            pl.BlockSpec(memory_space=pltpu.SMEM),
            pl.BlockSpec((tq, gw), qmap),
            pl.BlockSpec((seq, HEAD_DIM), kmap),
            pl.BlockSpec((seq, HEAD_DIM), kmap),
        ],
        out_specs=pl.BlockSpec((tq, gw), qmap),
        out_shape=jax.ShapeDtypeStruct((n, GQA_WIDTH), BF16),
        scratch_shapes=stat + [pltpu.VMEM((GQA_GROUP, tq, HEAD_DIM), F32)],
        compiler_params=_params("parallel", "parallel", "parallel"),
        name="gqa_attention" + ("_stable" if stable else ""),
    )(shift.reshape(1), bq, bk, bv)


def _gqa_attention(bq, bk, bv, bound, *, batch, seq):
    shift = bound * LOG2E
    return lax.cond(
        bound <= SHIFT_MAX,
        functools.partial(_gqa_attention_call, batch=batch, seq=seq, stable=False),
        functools.partial(_gqa_attention_call, batch=batch, seq=seq, stable=True),
        shift, bq, bk, bv)


def _out_proj_kernel(a_ref, b_ref, wa_ref, wb_ref, x_ref, o_ref):
    acc = jnp.dot(a_ref[...], wa_ref[...], preferred_element_type=F32)
    acc = acc + jnp.dot(b_ref[...], wb_ref[...], preferred_element_type=F32)
    o_ref[...] = x_ref[...] + acc


def _out_proj(ao, bo, w_out, x, *, tm=1024, tn=512):
    n, d = x.shape
    tm = min(tm, n)
    tn = min(tn, d)
    wa, wb = ao.shape[1], bo.shape[1]
    return pl.pallas_call(
        _out_proj_kernel,
        grid=(n // tm, d // tn),
        in_specs=[
            pl.BlockSpec((tm, wa), lambda i, j: (i, 0)),
            pl.BlockSpec((tm, wb), lambda i, j: (i, 0)),
            pl.BlockSpec((wa, tn), lambda i, j: (0, j)),
            pl.BlockSpec((wb, tn), lambda i, j: (wa // wb, j)),
            pl.BlockSpec((tm, tn), lambda i, j: (i, j)),
        ],
        out_specs=pl.BlockSpec((tm, tn), lambda i, j: (i, j)),
        out_shape=jax.ShapeDtypeStruct((n, d), F32),
        compiler_params=_params("parallel", "parallel"),
        name="out_proj",
    )(ao, bo, w_out, w_out, x)


def _swiglu_act(g, u):
    return (g * jax.nn.sigmoid(g)) * u


def _ffn_kernel(x_ref, gn_ref, wg_ref, wu_ref, wd_ref, o_ref, h_ref):
    @pl.when(pl.program_id(1) == 0)
    def _():
        xf = x_ref[...]
        h_ref[...] = _rms_rows(xf, gn_ref[...]).astype(h_ref.dtype)
        o_ref[...] = xf

    h = h_ref[...]
    g = jnp.dot(h, wg_ref[...], preferred_element_type=F32)
    u = jnp.dot(h, wu_ref[...], preferred_element_type=F32)
    a = _swiglu_act(g, u).astype(BF16)
    o_ref[...] += jnp.dot(a, wd_ref[...], preferred_element_type=F32)


def _ffn_dense(x, gn, wg, wu, wd, *, tm=512, tf=512):
    n, d = x.shape
    f = wg.shape[1]
    tm = min(tm, n)
    return pl.pallas_call(
        _ffn_kernel,
        grid=(n // tm, f // tf),
        in_specs=[
            pl.BlockSpec((tm, d), lambda i, j: (i, 0)),
            pl.BlockSpec((1, d), lambda i, j: (0, 0)),
            pl.BlockSpec((d, tf), lambda i, j: (0, j)),
            pl.BlockSpec((d, tf), lambda i, j: (0, j)),
            pl.BlockSpec((tf, d), lambda i, j: (j, 0)),
        ],
        out_specs=pl.BlockSpec((tm, d), lambda i, j: (i, 0)),
        out_shape=jax.ShapeDtypeStruct((n, d), F32),
        scratch_shapes=[pltpu.VMEM((tm, d), BF16)],
        compiler_params=_params("parallel", "arbitrary"),
        name="ffn_dense",
    )(x, gn.reshape(1, d), wg, wu, wd)


def _pad_ff(w, axis, mult):
    f = w.shape[axis]
    fp = -(-f // mult) * mult
    if fp == f:
        return w
    pad = [(0, 0)] * w.ndim
    pad[axis] = (0, fp - f)
    return jnp.pad(w, pad)


def _router_kernel(x_ref, gn_ref, r_ref, h_ref, idx_ref, gate_ref):
    h = _rms_rows(x_ref[...], gn_ref[...])
    h_ref[...] = h
    logits = jnp.dot(h.astype(BF16), r_ref[...], preferred_element_type=F32)
    lane = lax.broadcasted_iota(jnp.int32, logits.shape, 1)
    m1 = jnp.max(logits, axis=-1, keepdims=True)
    i1 = jnp.min(jnp.where(logits == m1, lane, N_EXPERTS), axis=-1, keepdims=True)
    rest = jnp.where(lane == i1, -jnp.inf, logits)
    m2 = jnp.max(rest, axis=-1, keepdims=True)
    i2 = jnp.min(jnp.where(rest == m2, lane, N_EXPERTS), axis=-1, keepdims=True)
    e2 = jnp.exp(m2 - m1)
    den = 1.0 + e2
    first = lax.broadcasted_iota(jnp.int32, idx_ref.shape, 1) == 0
    idx_ref[...] = jnp.where(first, i1, i2)
    gate_ref[...] = jnp.where(first, 1.0 / den, e2 / den)


def _router(x, gn, router, *, tm=512):
    n, d = x.shape
    tm = min(tm, n)
    return pl.pallas_call(
        _router_kernel,
        grid=(n // tm,),
        in_specs=[
            pl.BlockSpec((tm, d), lambda i: (i, 0)),
            pl.BlockSpec((1, d), lambda i: (0, 0)),
            pl.BlockSpec((d, N_EXPERTS), lambda i: (0, 0)),
        ],
        out_specs=[
            pl.BlockSpec((tm, d), lambda i: (i, 0)),
            pl.BlockSpec((tm, TOP_K), lambda i: (i, 0)),
            pl.BlockSpec((tm, TOP_K), lambda i: (i, 0)),
        ],
        out_shape=[
            jax.ShapeDtypeStruct((n, d), F32),
            jax.ShapeDtypeStruct((n, TOP_K), jnp.int32),
            jax.ShapeDtypeStruct((n, TOP_K), F32),
        ],
        compiler_params=_params("parallel"),
        name="moe_router",
    )(x, gn.reshape(1, d), router)


def _gather_rows(idx_ref, src_ref, dst_ref, sem, rows):
    def start(r, carry):
        pltpu.make_async_copy(src_ref.at[pl.ds(idx_ref[r], 1)], dst_ref.at[pl.ds(r, 1)], sem).start()
        return carry

    lax.fori_loop(0, rows, start, 0, unroll=8)
    pltpu.make_async_copy(src_ref.at[pl.ds(0, rows)], dst_ref.at[pl.ds(0, rows)], sem).wait()


def _gather_kernel(idx_ref, src_ref, o_ref, sem, *, rows):
    _gather_rows(idx_ref, src_ref, o_ref, sem, rows)


def _gather(src, idx, *, rows=512):
    r = idx.shape[0]
    d = src.shape[1]
    rows = min(rows, r)
    return pl.pallas_call(
        functools.partial(_gather_kernel, rows=rows),
        grid=(r // rows,),
        in_specs=[
            pl.BlockSpec((rows,), lambda i: (i,), memory_space=pltpu.SMEM),
            pl.BlockSpec(memory_space=pl.ANY),
        ],
        out_specs=pl.BlockSpec((rows, d), lambda i: (i, 0)),
        out_shape=jax.ShapeDtypeStruct((r, d), src.dtype),
        scratch_shapes=[pltpu.SemaphoreType.DMA],
        compiler_params=_params("arbitrary"),
        name="moe_gather",
    )(idx, src)


def _moe_ffn_kernel(te_ref, nu_ref, h_ref, gate_ref, wg_ref, wu_ref, wd_ref, o_ref, hb_ref):
    i = pl.program_id(0)
    j = pl.program_id(1)
    used = i < nu_ref[0]

    @pl.when(j == 0)
    def _():
        hb_ref[...] = h_ref[...].astype(BF16)
        o_ref[...] = jnp.zeros(o_ref.shape, F32)

    @pl.when(used)
    def _():
        h = hb_ref[...]
        g = jnp.dot(h, wg_ref[...], preferred_element_type=F32)
        u = jnp.dot(h, wu_ref[...], preferred_element_type=F32)
        a = (_swiglu_act(g, u) * gate_ref[...]).astype(BF16)
        o_ref[...] += jnp.dot(a, wd_ref[...], preferred_element_type=F32)


def _moe_ffn(tile_expert, n_used, h_sorted, gate_sorted, wg, wu, wd, *, tm, tf=512):
    r, d = h_sorted.shape
    f = wg.shape[2]
    nf = f // tf
    jj = lambda i, j, te, nu: jnp.where(i < nu[0], j, nf - 1)
    return pl.pallas_call(
        _moe_ffn_kernel,
        grid_spec=pltpu.PrefetchScalarGridSpec(
            num_scalar_prefetch=2,
            grid=(r // tm, nf),
            in_specs=[
                pl.BlockSpec((tm, d), lambda i, j, te, nu: (i, 0)),
                pl.BlockSpec((tm, 1), lambda i, j, te, nu: (i, 0)),
                pl.BlockSpec((None, d, tf), lambda i, j, te, nu: (te[i], 0, jj(i, j, te, nu))),
                pl.BlockSpec((None, d, tf), lambda i, j, te, nu: (te[i], 0, jj(i, j, te, nu))),
                pl.BlockSpec((None, tf, d), lambda i, j, te, nu: (te[i], jj(i, j, te, nu), 0)),
            ],
            out_specs=pl.BlockSpec((tm, d), lambda i, j, te, nu: (i, 0)),
            scratch_shapes=[pltpu.VMEM((tm, d), BF16)],
        ),
        out_shape=jax.ShapeDtypeStruct((r, d), F32),
        compiler_params=_params("arbitrary", "arbitrary"),
        name="moe_experts",
    )(tile_expert, n_used, h_sorted, gate_sorted, wg, wu, wd)


def _combine_kernel(idx_ref, y_ref, x_ref, o_ref, buf_ref, sem, *, tc):
    _gather_rows(idx_ref, y_ref, buf_ref, sem, TOP_K * tc)
    o_ref[...] = x_ref[...] + buf_ref[pl.ds(0, tc), :] + buf_ref[pl.ds(tc, tc), :]


def _combine(y_sorted, pos, x, *, tc=256):
    n, d = x.shape
    tc = min(tc, n)
    idx = pos.reshape(n // tc, tc, TOP_K).transpose(0, 2, 1).reshape(-1)
    return pl.pallas_call(
        functools.partial(_combine_kernel, tc=tc),
        grid=(n // tc,),
        in_specs=[
            pl.BlockSpec((TOP_K * tc,), lambda i: (i,), memory_space=pltpu.SMEM),
            pl.BlockSpec(memory_space=pl.ANY),
            pl.BlockSpec((tc, d), lambda i: (i, 0)),
        ],
        out_specs=pl.BlockSpec((tc, d), lambda i: (i, 0)),
        out_shape=jax.ShapeDtypeStruct((n, d), F32),
        scratch_shapes=[pltpu.VMEM((TOP_K * tc, d), F32), pltpu.SemaphoreType.DMA],
        compiler_params=_params("arbitrary"),
        name="moe_combine",
    )(idx, y_sorted, x)


def _route_plan(idx, gate, tm):
    n = idx.shape[0]
    a = n * TOP_K
    rows = a + N_EXPERTS * tm
    e_flat = idx.reshape(a)
    order = jnp.argsort(e_flat, stable=True).astype(jnp.int32)
    counts = jnp.sum(e_flat[:, None] == jnp.arange(N_EXPERTS)[None, :], axis=0).astype(jnp.int32)
    padded = ((counts + tm - 1) // tm) * tm
    pad_end = jnp.cumsum(padded)
    pad_start = pad_end - padded
    raw_start = jnp.cumsum(counts) - counts
    r = jnp.arange(rows, dtype=jnp.int32)
    tile_expert = jnp.minimum(jnp.sum(r[::tm, None] >= pad_end[None, :], axis=1), N_EXPERTS - 1).astype(jnp.int32)
    row_expert = jnp.repeat(tile_expert, tm, total_repeat_length=rows)
    rank = r - pad_start[row_expert]
    valid = rank < counts[row_expert]
    row_assign = order[jnp.clip(raw_start[row_expert] + rank, 0, a - 1)]
    row_token = jnp.where(valid, row_assign // TOP_K, 0)
    row_gate = jnp.where(valid, gate.reshape(a)[row_assign], 0.0)
    sorted_pos = jnp.argsort(order).astype(jnp.int32)
    pos = pad_start[e_flat] + sorted_pos - raw_start[e_flat]
    n_used = (pad_end[-1] // tm).reshape(1).astype(jnp.int32)
    return row_token, row_gate.reshape(rows, 1), pos.reshape(n, TOP_K), tile_expert, n_used


def _moe(x, gn, router, wg, wu, wd, *, tm=512):
    n = x.shape[0]
    tm = min(tm, n)
    h, idx, gate = _router(x, gn, router)
    row_token, row_gate, pos, tile_expert, n_used = _route_plan(idx, gate, tm)
    h_sorted = _gather(h, row_token, rows=tm)
    y_sorted = _moe_ffn(tile_expert, n_used, h_sorted, row_gate, wg, wu, wd, tm=tm)
    return _combine(y_sorted, pos, x)


def kernel(x, attn_norm, w_in, diff_q_norm, diff_k_norm, diff_lambda, diff_subln, gqa_q_norm, gqa_k_norm, w_out, ffn_norm, dense_w_gate, dense_w_up, dense_w_down, moe_router, moe_w_gate, moe_w_up, moe_w_down):
    batch, seq, d = x.shape
    depth = w_in.shape[0]
    n = batch * seq
    xs = x.reshape(n, d).astype(F32)
    tables = _position_tables(seq, min(ATTN_TILE, seq))
    slopes = tuple(2.0 ** (-8.0 * (h + 1) / DIFF_HEADS) for h in range(DIFF_HEADS))
    ff_tile = 512
    bound_margin = 1.02

    for l in range(depth):
        proj = _norm_matmul(xs, attn_norm[l], w_in[l].astype(BF16))
        amax = lambda g: jnp.max(jnp.abs(g.astype(F32)))
        diff_bound = amax(diff_q_norm[l]) * amax(diff_k_norm[l]) * (DIFF_HALF ** 0.5 * bound_margin)
        gqa_bound = amax(gqa_q_norm[l]) * amax(gqa_k_norm[l]) * (HEAD_DIM ** 0.5 * bound_margin)
        diff_ok = diff_bound <= SHIFT_MAX
        diff_shift = jnp.where(diff_ok, diff_bound, 0.0)
        aq1, aq2, ak1, ak2, av, bq, bk, bv = _prep(
            proj, tables, diff_shift, diff_q_norm[l], diff_k_norm[l], gqa_q_norm[l], gqa_k_norm[l],
            seq=seq, slopes=slopes)
        lam_init = 0.8 - 0.6 * math.exp(-0.3 * l)
        lp = diff_lambda[l].astype(F32)
        lam = jnp.exp(jnp.sum(lp[0] * lp[1])) - jnp.exp(jnp.sum(lp[2] * lp[3])) + lam_init
        ao = _diff_attention(aq1, aq2, ak1, ak2, av, lam, diff_subln[l], diff_ok,
                             batch=batch, seq=seq, out_scale=1.0 - lam_init)
        bo = _gqa_attention(bq, bk, bv, gqa_bound, batch=batch, seq=seq)
        xs = _out_proj(ao, bo, w_out[l].astype(BF16), xs)

        i = l // 2
        if l % 2 == 0:
            wg = _pad_ff(dense_w_gate[i].astype(BF16), 1, ff_tile)
            wu = _pad_ff(dense_w_up[i].astype(BF16), 1, ff_tile)
            wd = _pad_ff(dense_w_down[i].astype(BF16), 0, ff_tile)
            xs = _ffn_dense(xs, ffn_norm[l], wg, wu, wd, tf=ff_tile)
        else:
            xs = _moe(xs, ffn_norm[l], moe_router[i].astype(BF16), moe_w_gate[i].astype(BF16),
                      moe_w_up[i].astype(BF16), moe_w_down[i].astype(BF16))
    return xs.reshape(batch, seq, d)
```

```python
import functools
import math

import numpy as np
import jax
import jax.numpy as jnp
from jax import lax
from jax.experimental import pallas as pl
from jax.experimental.pallas import tpu as pltpu

HEAD_DIM = 128
DIFF_HALF = HEAD_DIM // 2
DIFF_HEADS = 8
GQA_HEADS = 8
GQA_KV_HEADS = 2
GQA_GROUP = GQA_HEADS // GQA_KV_HEADS
DIFF_WIDTH = DIFF_HEADS * HEAD_DIM
GQA_WIDTH = GQA_HEADS * HEAD_DIM
GQA_KV_WIDTH = GQA_KV_HEADS * HEAD_DIM
GRID_W = 64
ROPE_THETA = 10000.0
ROPE_AXIS_DIM = HEAD_DIM // 2
N_EXPERTS = 8
TOP_K = 2
EPS = 1e-6

LANES = 128
VMEM_LIMIT_BYTES = 52 * 1024 * 1024

BF16 = jnp.bfloat16
F32 = jnp.float32


def _params(*sem):
    return pltpu.CompilerParams(dimension_semantics=sem, vmem_limit_bytes=VMEM_LIMIT_BYTES)


def _rms_rows(xf, g):
    ms = jnp.mean(xf * xf, axis=-1, keepdims=True)
    return (xf * lax.rsqrt(ms + EPS)) * g


ATTN_TILE = 512
DIFF_QSUB = 4
GQA_QTILE = 1024
GQA_KTILE = 1024
AUG0 = DIFF_HALF
LOG2E = 1.4426950408889634


IN_PROJ_TN = 512


def _in_proj_kernel(shift_ref, x_ref, g_ref, w_ref, cos_ref, sin_ref, qaug_ref, kaug_ref,
                    dq_ref, dk_ref, gq_ref, gk_ref,
                    aq1_ref, aq2_ref, ak1_ref, ak2_ref, av_ref, bq_ref, bk_ref, bv_ref, n_ref, *, slopes):
    j = pl.program_id(1)

    @pl.when(j == 0)
    def _():
        n_ref[...] = _rms_rows(x_ref[...], g_ref[...]).astype(n_ref.dtype)

    acc = jnp.dot(n_ref[...], w_ref[...], preferred_element_type=F32)
    heads_per_tile = IN_PROJ_TN // HEAD_DIM
    chunk = lambda c: acc[:, c * HEAD_DIM:(c + 1) * HEAD_DIM]
    lanes = lambda c: slice(c * HEAD_DIM, (c + 1) * HEAD_DIM)

    lane = lax.broadcasted_iota(jnp.int32, (1, HEAD_DIM), 1)
    lo = lane < DIFF_HALF
    even = (lane & 1) == 0

    def half_rms(x, g):
        sq = x * x
        s_lo = jnp.sum(jnp.where(lo, sq, 0.0), axis=-1, keepdims=True)
        s_hi = jnp.sum(jnp.where(lo, 0.0, sq), axis=-1, keepdims=True)
        ms = jnp.where(lo, s_lo, s_hi) * (1.0 / DIFF_HALF)
        return (x * lax.rsqrt(ms + EPS)) * g

    def rope(y):
        sw = jnp.where(even, pltpu.roll(y, HEAD_DIM - 1, 1), pltpu.roll(y, 1, 1))
        return y * cos_ref[...] + sw * sin_ref[...]

    for jq in range(DIFF_HEADS // heads_per_tile):
        @pl.when(j == jq)
        def _():
            tail = jnp.where(lane == AUG0 + 4, 1.0, 0.0) + jnp.where(lane == AUG0 + 5, -shift_ref[0], 0.0)
            for c in range(heads_per_tile):
                q = half_rms(chunk(c), dq_ref[...]) * (DIFF_HALF ** -0.5)
                qa = qaug_ref[...] * slopes[jq * heads_per_tile + c] + tail
                aq1_ref[:, lanes(c)] = jnp.where(lo, q, qa).astype(BF16)
                aq2_ref[:, lanes(c)] = jnp.where(lo, pltpu.roll(q, DIFF_HALF, 1), qa).astype(BF16)

    @pl.when((j == 2) | (j == 3))
    def _():
        kaug = kaug_ref[...]
        for c in range(heads_per_tile):
            k = half_rms(chunk(c), dk_ref[...])
            ak1_ref[:, lanes(c)] = jnp.where(lo, k, kaug).astype(BF16)
            ak2_ref[:, lanes(c)] = jnp.where(lo, pltpu.roll(k, DIFF_HALF, 1), kaug).astype(BF16)

    @pl.when((j == 4) | (j == 5))
    def _():
        av_ref[...] = acc.astype(BF16)

    @pl.when((j == 6) | (j == 7))
    def _():
        for c in range(heads_per_tile):
            q = rope(_rms_rows(chunk(c), gq_ref[...])) * (HEAD_DIM ** -0.5 * LOG2E)
            bq_ref[:, lanes(c)] = q.astype(BF16)

    @pl.when(j == 8)
    def _():
        for c in range(GQA_KV_HEADS):
            bk_ref[:, lanes(c)] = rope(_rms_rows(chunk(c), gk_ref[...])).astype(BF16)
            bv_ref[:, lanes(c)] = chunk(GQA_KV_HEADS + c).astype(BF16)


def _in_proj(x, g, w, tables, shift, dq, dk, gq, gk, *, seq, slopes, tm=1024):
    n, d = x.shape
    tn = IN_PROJ_TN
    assert w.shape[1] == 3 * DIFF_WIDTH + GQA_WIDTH + 2 * GQA_KV_WIDTH and 2 * GQA_KV_WIDTH == tn
    tm = min(tm, seq)
    nt = seq // tm
    pos = lambda i, j: (i % nt, 0)
    fixed = lambda i, j: (0, 0)
    wide = lambda j0: (lambda i, j: (i, jnp.clip(j - j0, 0, 1)))
    dq2 = jnp.concatenate([dq, dq]).reshape(1, HEAD_DIM)
    dk2 = jnp.concatenate([dk, dk]).reshape(1, HEAD_DIM)
    kv = (n, GQA_KV_WIDTH)
    shapes = [(n, DIFF_WIDTH)] * 5 + [(n, GQA_WIDTH), kv, kv]
    starts = [0, 0, 2, 2, 4, 6]
    return pl.pallas_call(
        functools.partial(_in_proj_kernel, slopes=slopes),
        grid=(n // tm, w.shape[1] // tn),
        in_specs=[
            pl.BlockSpec(memory_space=pltpu.SMEM),
            pl.BlockSpec((tm, d), lambda i, j: (i, 0)),
            pl.BlockSpec((1, d), fixed),
            pl.BlockSpec((d, tn), lambda i, j: (0, j)),
        ] + [pl.BlockSpec((tm, HEAD_DIM), pos)] * 4 + [pl.BlockSpec((1, HEAD_DIM), fixed)] * 4,
        out_specs=[pl.BlockSpec((tm, tn), wide(j0)) for j0 in starts]
        + [pl.BlockSpec((tm, GQA_KV_WIDTH), lambda i, j: (i, 0))] * 2,
        out_shape=[jax.ShapeDtypeStruct(sh, BF16) for sh in shapes],
        scratch_shapes=[pltpu.VMEM((tm, d), BF16)],
        compiler_params=_params("parallel", "arbitrary"),
        name="in_proj_heads",
    )(shift.reshape(1), x, g.reshape(1, d), w, *tables, dq2, dk2, gq.reshape(1, HEAD_DIM), gk.reshape(1, HEAD_DIM))


def _position_tables(seq, tile):
    rows_n = seq // GRID_W
    row = jnp.repeat(jnp.arange(rows_n, dtype=F32), GRID_W, total_repeat_length=seq)
    col = jnp.tile(jnp.arange(GRID_W, dtype=F32), rows_n)
    inv = ROPE_THETA ** (-jnp.arange(0, ROPE_AXIS_DIM, 2, dtype=F32) / ROPE_AXIS_DIM)
    ang = jnp.concatenate([row[:, None] * inv, col[:, None] * inv], axis=-1)
    cos = jnp.repeat(jnp.cos(ang), 2, axis=-1)
    sin = jnp.sin(ang)
    sin_signed = jnp.stack([-sin, sin], axis=-1).reshape(seq, HEAD_DIM)
    loc = np.arange(seq) % tile
    hi = (loc // 16).astype(np.float32)
    lo = (loc % 16).astype(np.float32)
    qaug = np.zeros((seq, HEAD_DIM), np.float32)
    kaug = np.zeros((seq, HEAD_DIM), np.float32)
    qaug[:, AUG0 + 0] = 16.0
    qaug[:, AUG0 + 1] = 1.0
    qaug[:, AUG0 + 2] = -16.0 * hi
    qaug[:, AUG0 + 3] = -lo
    kaug[:, AUG0 + 0] = hi
    kaug[:, AUG0 + 1] = lo
    kaug[:, AUG0 + 2:AUG0 + 6] = 1.0
    return cos, sin_signed, jnp.asarray(qaug), jnp.asarray(kaug)


SHIFT_MAX = 40.0
UNDERFLOW = 106.0


def _qk(q, k):
    return lax.dot_general(q, k, (((1,), (1,)), ((), ())), preferred_element_type=F32)


def _lane_fold(p):
    out = p[:, :LANES]
    for c in range(1, p.shape[1] // LANES):
        out = out + p[:, c * LANES:(c + 1) * LANES]
    return out


def _online_update(s, v, m_ref, l_ref, acc_ref, idx, exp_fn):
    m_old = m_ref[idx]
    m_new = jnp.maximum(m_old, jnp.max(s, axis=-1, keepdims=True))
    alpha = exp_fn(m_old - m_new)
    p = exp_fn(s - m_new)
    l_ref[idx] = alpha * l_ref[idx] + jnp.sum(p, axis=-1, keepdims=True)
    acc_ref[idx] = alpha * acc_ref[idx] + jnp.dot(p.astype(BF16), v, preferred_element_type=F32)
    m_ref[idx] = m_new


def _shifted_update(p, v, l_ref, acc_ref, idx):
    l_ref[idx] += _lane_fold(p)
    acc_ref[idx] += jnp.dot(p.astype(BF16), v, preferred_element_type=F32)


def _row_total(l_ref, idx, stable):
    return l_ref[idx] if stable else jnp.sum(l_ref[idx], axis=-1, keepdims=True)


def _diff_attn_kernel(lam_ref, q1_ref, q2_ref, k1_ref, k2_ref, v_ref, g_ref, o_ref, *scratch,
                      slope, t, qsub, nk, window, out_scale, stable):
    if stable:
        m_ref, l_ref, acc_ref = scratch
        m_ref[...] = jnp.full(m_ref.shape, -jnp.inf, F32)
    else:
        l_ref, acc_ref = scratch
    l_ref[...] = jnp.zeros(l_ref.shape, F32)
    acc_ref[...] = jnp.zeros(acc_ref.shape, F32)
    q0 = qsub * pl.program_id(1)

    def tile(sub, d, ki, bias):
        rows = pl.ds(sub * t, t)
        keys = pl.ds(pl.multiple_of(ki * t, t), t)
        lane = lax.broadcasted_iota(jnp.int32, (1, HEAD_DIM), 1)
        sig = jnp.where(d > 0, 1.0, jnp.where(d < 0, -1.0, 0.0))
        tval = (-slope * t) * jnp.abs(d).astype(F32)
        mult = jnp.where(lane < AUG0, 1.0,
                         jnp.where(lane < AUG0 + 4, sig, jnp.where(lane == AUG0 + 4, tval, 1.0))).astype(BF16)
        v = v_ref[keys, :]
        for c, (q_ref, k_ref) in enumerate(((q1_ref, k1_ref), (q2_ref, k2_ref))):
            s = _qk(q_ref[rows, :] * mult, k_ref[keys, :])
            if bias is not None:
                s = s + bias
            if stable:
                _online_update(s, v, m_ref, l_ref, acc_ref, qsub * c + sub, jnp.exp)
            else:
                _shifted_update(jnp.exp(s), v, l_ref, acc_ref, qsub * c + sub)

    def key_tile(ki, carry):
        offs = [q0 + sub - ki for sub in range(qsub)]
        offdiag = [(d != 0) if window is None else (d != 0) & (jnp.abs(d) <= window) for d in offs]
        all_off = functools.reduce(jnp.logical_and, offdiag)

        @pl.when(all_off)
        def _():
            for sub, d in enumerate(offs):
                tile(sub, d, ki, None)

        for sub, d in enumerate(offs):
            @pl.when(offdiag[sub] & jnp.logical_not(all_off))
            def _():
                tile(sub, d, ki, None)

            @pl.when(d == 0)
            def _():
                r = lax.broadcasted_iota(jnp.int32, (t, t), 0)
                c = lax.broadcasted_iota(jnp.int32, (t, t), 1)
                tile(sub, d, ki, jnp.abs(r - c).astype(F32) * (-slope))
        return carry

    if window is None:
        lo, hi = 0, nk
    else:
        lo = jnp.maximum(q0 - window, 0)
        hi = jnp.minimum(q0 + qsub + window, nk)
    lax.fori_loop(lo, hi, key_tile, 0)

    for sub in range(qsub):
        o1 = acc_ref[sub] / _row_total(l_ref, sub, stable)
        o2 = acc_ref[qsub + sub] / _row_total(l_ref, qsub + sub, stable)
        o = _rms_rows(o1 - lam_ref[0] * o2, g_ref[...]) * out_scale
        o_ref[pl.ds(sub * t, t), :] = o.astype(o_ref.dtype)


def _diff_window(slope, tile, nk):
    w = int(math.floor((UNDERFLOW / slope - 1.0) / tile)) + 1
    return None if 2 * w + 1 >= nk else w


def _diff_attention_head(h, aq1, aq2, ak1, ak2, av, lam, subln, *, batch, seq, out_scale, stable):
    n = aq1.shape[0]
    t = min(ATTN_TILE, seq)
    nk = seq // t
    qsub = DIFF_QSUB if nk % DIFF_QSUB == 0 else 1
    nqb = nk // qsub
    slope = 2.0 ** (-8.0 * (h + 1) / DIFF_HEADS)
    window = None if stable else _diff_window(slope, t, nk)
    qmap = lambda b, qb: (b * nqb + qb, h)
    kmap = lambda b, qb: (b, h)
    kern = functools.partial(_diff_attn_kernel, slope=slope, t=t, qsub=qsub, nk=nk, window=window,
                             out_scale=out_scale, stable=stable)
    stat = [pltpu.VMEM((2 * qsub, t, 1), F32)] * 2 if stable else [pltpu.VMEM((2 * qsub, t, LANES), F32)]
    return pl.pallas_call(
        kern,
        grid=(batch, nqb),
        in_specs=[
            pl.BlockSpec(memory_space=pltpu.SMEM),
            pl.BlockSpec((qsub * t, HEAD_DIM), qmap),
            pl.BlockSpec((qsub * t, HEAD_DIM), qmap),
            pl.BlockSpec((seq, HEAD_DIM), kmap),
            pl.BlockSpec((seq, HEAD_DIM), kmap),
            pl.BlockSpec((seq, HEAD_DIM), kmap),
            pl.BlockSpec((1, HEAD_DIM), lambda b, qb: (0, 0)),
        ],
        out_specs=pl.BlockSpec((qsub * t, HEAD_DIM), lambda b, qb: (b * nqb + qb, 0)),
        out_shape=jax.ShapeDtypeStruct((n, HEAD_DIM), BF16),
        scratch_shapes=stat + [pltpu.VMEM((2 * qsub, t, HEAD_DIM), F32)],
        compiler_params=_params("parallel", "parallel"),
        name=f"diff_attention_h{h}" + ("_stable" if stable else ""),
    )(lam.reshape(1), aq1, aq2, ak1, ak2, av, subln.reshape(1, HEAD_DIM))


def _diff_attention(aq1, aq2, ak1, ak2, av, lam, subln, bound_ok, *, batch, seq, out_scale):
    def run(stable):
        def f(aq1, aq2, ak1, ak2, av, lam, subln):
            heads = [_diff_attention_head(h, aq1, aq2, ak1, ak2, av, lam, subln, batch=batch, seq=seq,
                                          out_scale=out_scale, stable=stable) for h in range(DIFF_HEADS)]
            return jnp.concatenate(heads, axis=-1)
        return f
    return lax.cond(bound_ok, run(False), run(True), aq1, aq2, ak1, ak2, av, lam, subln)


def _gqa_attn_kernel(shift_ref, q_ref, k_ref, v_ref, o_ref, *scratch, tk, nk, stable):
    if stable:
        m_ref, l_ref, acc_ref = scratch
        m_ref[...] = jnp.full(m_ref.shape, -jnp.inf, F32)
    else:
        l_ref, acc_ref = scratch
    l_ref[...] = jnp.zeros(l_ref.shape, F32)
    acc_ref[...] = jnp.zeros(acc_ref.shape, F32)

    def key_tile(ki, carry):
        keys = pl.ds(pl.multiple_of(ki * tk, tk), tk)
        k = k_ref[keys, :]
        v = v_ref[keys, :]
        for r in range(GQA_GROUP):
            s = _qk(q_ref[:, r * HEAD_DIM:(r + 1) * HEAD_DIM], k)
            if stable:
                _online_update(s, v, m_ref, l_ref, acc_ref, r, jnp.exp2)
            else:
                _shifted_update(jnp.exp2(s - shift_ref[0]), v, l_ref, acc_ref, r)
        return carry

    lax.fori_loop(0, nk, key_tile, 0)

    for r in range(GQA_GROUP):
        o = acc_ref[r] / _row_total(l_ref, r, stable)
        o_ref[:, r * HEAD_DIM:(r + 1) * HEAD_DIM] = o.astype(o_ref.dtype)


def _gqa_attention_call(shift, bq, bk, bv, *, batch, seq, stable):
    n = bq.shape[0]
    tk = min(GQA_KTILE, seq)
    tq = min(GQA_QTILE, seq)
    nq, nk = seq // tq, seq // tk
    gw = GQA_GROUP * HEAD_DIM
    qmap = lambda b, g, qi: (b * nq + qi, g)
    kmap = lambda b, g, qi: (b, g)
    stat = [pltpu.VMEM((GQA_GROUP, tq, 1), F32)] * 2 if stable else [pltpu.VMEM((GQA_GROUP, tq, LANES), F32)]
    return pl.pallas_call(
        functools.partial(_gqa_attn_kernel, tk=tk, nk=nk, stable=stable),
        grid=(batch, GQA_KV_HEADS, nq),
        in_specs=[
            pl.BlockSpec(memory_space=pltpu.SMEM),
            pl.BlockSpec((tq, gw), qmap),
            pl.BlockSpec((seq, HEAD_DIM), kmap),
            pl.BlockSpec((seq, HEAD_DIM), kmap),
        ],
        out_specs=pl.BlockSpec((tq, gw), qmap),
        out_shape=jax.ShapeDtypeStruct((n, GQA_WIDTH), BF16),
        scratch_shapes=stat + [pltpu.VMEM((GQA_GROUP, tq, HEAD_DIM), F32)],
        compiler_params=_params("parallel", "parallel", "parallel"),
        name="gqa_attention" + ("_stable" if stable else ""),
    )(shift.reshape(1), bq, bk, bv)


def _gqa_attention(bq, bk, bv, bound, *, batch, seq):
    shift = bound * LOG2E
    return lax.cond(
        bound <= SHIFT_MAX,
        functools.partial(_gqa_attention_call, batch=batch, seq=seq, stable=False),
        functools.partial(_gqa_attention_call, batch=batch, seq=seq, stable=True),
        shift, bq, bk, bv)


def _out_proj_kernel(a_ref, b_ref, wa_ref, wb_ref, x_ref, o_ref):
    acc = jnp.dot(a_ref[...], wa_ref[...], preferred_element_type=F32)
    acc = acc + jnp.dot(b_ref[...], wb_ref[...], preferred_element_type=F32)
    o_ref[...] = x_ref[...] + acc


def _out_proj(ao, bo, w_out, x, *, tm=1024, tn=512):
    n, d = x.shape
    tm = min(tm, n)
    tn = min(tn, d)
    wa, wb = ao.shape[1], bo.shape[1]
    return pl.pallas_call(
        _out_proj_kernel,
        grid=(n // tm, d // tn),
        in_specs=[
            pl.BlockSpec((tm, wa), lambda i, j: (i, 0)),
            pl.BlockSpec((tm, wb), lambda i, j: (i, 0)),
            pl.BlockSpec((wa, tn), lambda i, j: (0, j)),
            pl.BlockSpec((wb, tn), lambda i, j: (wa // wb, j)),
            pl.BlockSpec((tm, tn), lambda i, j: (i, j)),
        ],
        out_specs=pl.BlockSpec((tm, tn), lambda i, j: (i, j)),
        out_shape=jax.ShapeDtypeStruct((n, d), F32),
        compiler_params=_params("parallel", "parallel"),
        name="out_proj",
    )(ao, bo, w_out, w_out, x)


def _swiglu_act(g, u):
    return (g * jax.nn.sigmoid(g)) * u


def _ffn_kernel(x_ref, gn_ref, wg_ref, wu_ref, wd_ref, o_ref, h_ref):
    @pl.when(pl.program_id(1) == 0)
    def _():
        xf = x_ref[...]
        h_ref[...] = _rms_rows(xf, gn_ref[...]).astype(h_ref.dtype)
        o_ref[...] = xf

    h = h_ref[...]
    g = jnp.dot(h, wg_ref[...], preferred_element_type=F32)
    u = jnp.dot(h, wu_ref[...], preferred_element_type=F32)
    a = _swiglu_act(g, u).astype(BF16)
    o_ref[...] += jnp.dot(a, wd_ref[...], preferred_element_type=F32)


def _ffn_dense(x, gn, wg, wu, wd, *, tm=512, tf=512):
    n, d = x.shape
    f = wg.shape[1]
    tm = min(tm, n)
    return pl.pallas_call(
        _ffn_kernel,
        grid=(n // tm, f // tf),
        in_specs=[
            pl.BlockSpec((tm, d), lambda i, j: (i, 0)),
            pl.BlockSpec((1, d), lambda i, j: (0, 0)),
            pl.BlockSpec((d, tf), lambda i, j: (0, j)),
            pl.BlockSpec((d, tf), lambda i, j: (0, j)),
            pl.BlockSpec((tf, d), lambda i, j: (j, 0)),
        ],
        out_specs=pl.BlockSpec((tm, d), lambda i, j: (i, 0)),
        out_shape=jax.ShapeDtypeStruct((n, d), F32),
        scratch_shapes=[pltpu.VMEM((tm, d), BF16)],
        compiler_params=_params("parallel", "arbitrary"),
        name="ffn_dense",
    )(x, gn.reshape(1, d), wg, wu, wd)


def _pad_ff(w, axis, mult):
    f = w.shape[axis]
    fp = -(-f // mult) * mult
    if fp == f:
        return w
    pad = [(0, 0)] * w.ndim
    pad[axis] = (0, fp - f)
    return jnp.pad(w, pad)


def _router_kernel(x_ref, gn_ref, r_ref, h_ref, idx_ref, gate_ref):
    h = _rms_rows(x_ref[...], gn_ref[...])
    h_ref[...] = h
    logits = jnp.dot(h.astype(BF16), r_ref[...], preferred_element_type=F32)
    lane = lax.broadcasted_iota(jnp.int32, logits.shape, 1)
    m1 = jnp.max(logits, axis=-1, keepdims=True)
    i1 = jnp.min(jnp.where(logits == m1, lane, N_EXPERTS), axis=-1, keepdims=True)
    rest = jnp.where(lane == i1, -jnp.inf, logits)
    m2 = jnp.max(rest, axis=-1, keepdims=True)
    i2 = jnp.min(jnp.where(rest == m2, lane, N_EXPERTS), axis=-1, keepdims=True)
    e2 = jnp.exp(m2 - m1)
    den = 1.0 + e2
    first = lax.broadcasted_iota(jnp.int32, idx_ref.shape, 1) == 0
    idx_ref[...] = jnp.where(first, i1, i2)
    gate_ref[...] = jnp.where(first, 1.0 / den, e2 / den)


def _router(x, gn, router, *, tm=512):
    n, d = x.shape
    tm = min(tm, n)
    return pl.pallas_call(
        _router_kernel,
        grid=(n // tm,),
        in_specs=[
            pl.BlockSpec((tm, d), lambda i: (i, 0)),
            pl.BlockSpec((1, d), lambda i: (0, 0)),
            pl.BlockSpec((d, N_EXPERTS), lambda i: (0, 0)),
        ],
        out_specs=[
            pl.BlockSpec((tm, d), lambda i: (i, 0)),
            pl.BlockSpec((tm, TOP_K), lambda i: (i, 0)),
            pl.BlockSpec((tm, TOP_K), lambda i: (i, 0)),
        ],
        out_shape=[
            jax.ShapeDtypeStruct((n, d), F32),
            jax.ShapeDtypeStruct((n, TOP_K), jnp.int32),
            jax.ShapeDtypeStruct((n, TOP_K), F32),
        ],
        compiler_params=_params("parallel"),
        name="moe_router",
    )(x, gn.reshape(1, d), router)


def _gather_rows(idx_ref, src_ref, dst_ref, sem, rows):
    def start(r, carry):
        pltpu.make_async_copy(src_ref.at[pl.ds(idx_ref[r], 1)], dst_ref.at[pl.ds(r, 1)], sem).start()
        return carry

    lax.fori_loop(0, rows, start, 0, unroll=8)
    pltpu.make_async_copy(src_ref.at[pl.ds(0, rows)], dst_ref.at[pl.ds(0, rows)], sem).wait()


def _gather_kernel(idx_ref, src_ref, o_ref, buf_ref, sem, *, rows):
    _gather_rows(idx_ref, src_ref, buf_ref, sem, rows)
    o_ref[...] = buf_ref[...].astype(o_ref.dtype)


def _gather(src, idx, *, rows, out_dtype):
    r = idx.shape[0]
    d = src.shape[1]
    return pl.pallas_call(
        functools.partial(_gather_kernel, rows=rows),
        grid=(r // rows,),
        in_specs=[
            pl.BlockSpec((rows,), lambda i: (i,), memory_space=pltpu.SMEM),
            pl.BlockSpec(memory_space=pl.ANY),
        ],
        out_specs=pl.BlockSpec((rows, d), lambda i: (i, 0)),
        out_shape=jax.ShapeDtypeStruct((r, d), out_dtype),
        scratch_shapes=[pltpu.VMEM((rows, d), src.dtype), pltpu.SemaphoreType.DMA],
        compiler_params=_params("arbitrary"),
        name="moe_gather",
    )(idx, src)


def _moe_ffn_kernel(te_ref, nu_ref, h_ref, gate_ref, wg_ref, wu_ref, wd_ref, o_ref):
    i = pl.program_id(0)
    j = pl.program_id(1)

    @pl.when(j == 0)
    def _():
        o_ref[...] = jnp.zeros(o_ref.shape, F32)

    @pl.when(i < nu_ref[0])
    def _():
        h = h_ref[...]
        g = jnp.dot(h, wg_ref[...], preferred_element_type=F32)
        u = jnp.dot(h, wu_ref[...], preferred_element_type=F32)
        a = (_swiglu_act(g, u) * gate_ref[...]).astype(BF16)
        o_ref[...] += jnp.dot(a, wd_ref[...], preferred_element_type=F32)


def _moe_ffn(tile_expert, n_used, h_sorted, gate_sorted, wg, wu, wd, *, tm, tf=1024):
    r, d = h_sorted.shape
    f = wg.shape[2]
    nf = f // tf
    jj = lambda i, j, te, nu: jnp.where(i < nu[0], j, nf - 1)
    return pl.pallas_call(
        _moe_ffn_kernel,
        grid_spec=pltpu.PrefetchScalarGridSpec(
            num_scalar_prefetch=2,
            grid=(r // tm, nf),
            in_specs=[
                pl.BlockSpec((tm, d), lambda i, j, te, nu: (i, 0)),
                pl.BlockSpec((tm, 1), lambda i, j, te, nu: (i, 0)),
                pl.BlockSpec((None, d, tf), lambda i, j, te, nu: (te[i], 0, jj(i, j, te, nu))),
                pl.BlockSpec((None, d, tf), lambda i, j, te, nu: (te[i], 0, jj(i, j, te, nu))),
                pl.BlockSpec((None, tf, d), lambda i, j, te, nu: (te[i], jj(i, j, te, nu), 0)),
            ],
            out_specs=pl.BlockSpec((tm, d), lambda i, j, te, nu: (i, 0)),
        ),
        out_shape=jax.ShapeDtypeStruct((r, d), F32),
        compiler_params=_params("arbitrary", "arbitrary"),
        name="moe_experts",
    )(tile_expert, n_used, h_sorted, gate_sorted, wg, wu, wd)


def _combine_kernel(idx_ref, y_ref, x_ref, o_ref, buf_ref, sem, *, tc):
    _gather_rows(idx_ref, y_ref, buf_ref, sem, TOP_K * tc)
    o_ref[...] = x_ref[...] + buf_ref[pl.ds(0, tc), :] + buf_ref[pl.ds(tc, tc), :]


def _combine(y_sorted, pos, x, *, tc=256):
    n, d = x.shape
    tc = min(tc, n)
    idx = pos.reshape(n // tc, tc, TOP_K).transpose(0, 2, 1).reshape(-1)
    return pl.pallas_call(
        functools.partial(_combine_kernel, tc=tc),
        grid=(n // tc,),
        in_specs=[
            pl.BlockSpec((TOP_K * tc,), lambda i: (i,), memory_space=pltpu.SMEM),
            pl.BlockSpec(memory_space=pl.ANY),
            pl.BlockSpec((tc, d), lambda i: (i, 0)),
        ],
        out_specs=pl.BlockSpec((tc, d), lambda i: (i, 0)),
        out_shape=jax.ShapeDtypeStruct((n, d), F32),
        scratch_shapes=[pltpu.VMEM((TOP_K * tc, d), F32), pltpu.SemaphoreType.DMA],
        compiler_params=_params("arbitrary"),
        name="moe_combine",
    )(idx, y_sorted, x)


def _route_plan(idx, gate, tm):
    n = idx.shape[0]
    a = n * TOP_K
    rows = a + N_EXPERTS * tm
    e_flat = idx.reshape(a)
    order = jnp.argsort(e_flat, stable=True).astype(jnp.int32)
    counts = jnp.sum(e_flat[:, None] == jnp.arange(N_EXPERTS)[None, :], axis=0).astype(jnp.int32)
    padded = ((counts + tm - 1) // tm) * tm
    pad_end = jnp.cumsum(padded)
    pad_start = pad_end - padded
    raw_start = jnp.cumsum(counts) - counts
    r = jnp.arange(rows, dtype=jnp.int32)
    tile_expert = jnp.minimum(jnp.sum(r[::tm, None] >= pad_end[None, :], axis=1), N_EXPERTS - 1).astype(jnp.int32)
    row_expert = jnp.repeat(tile_expert, tm, total_repeat_length=rows)
    rank = r - pad_start[row_expert]
    valid = rank < counts[row_expert]
    row_assign = order[jnp.clip(raw_start[row_expert] + rank, 0, a - 1)]
    row_token = jnp.where(valid, row_assign // TOP_K, 0)
    row_gate = jnp.where(valid, gate.reshape(a)[row_assign], 0.0)
    sorted_pos = jnp.argsort(order).astype(jnp.int32)
    pos = pad_start[e_flat] + sorted_pos - raw_start[e_flat]
    n_used = (pad_end[-1] // tm).reshape(1).astype(jnp.int32)
    return row_token, row_gate.reshape(rows, 1), pos.reshape(n, TOP_K), tile_expert, n_used


def _moe(x, gn, router, wg, wu, wd, *, tm=512):
    n = x.shape[0]
    tm = min(tm, n)
    h, idx, gate = _router(x, gn, router)
    row_token, row_gate, pos, tile_expert, n_used = _route_plan(idx, gate, tm)
    h_sorted = _gather(h, row_token, rows=tm, out_dtype=BF16)
    y_sorted = _moe_ffn(tile_expert, n_used, h_sorted, row_gate, wg, wu, wd, tm=tm)
    return _combine(y_sorted, pos, x)


def kernel(x, attn_norm, w_in, diff_q_norm, diff_k_norm, diff_lambda, diff_subln, gqa_q_norm, gqa_k_norm, w_out, ffn_norm, dense_w_gate, dense_w_up, dense_w_down, moe_router, moe_w_gate, moe_w_up, moe_w_down):
    batch, seq, d = x.shape
    depth = w_in.shape[0]
    n = batch * seq
    xs = x.reshape(n, d).astype(F32)
    tables = _position_tables(seq, min(ATTN_TILE, seq))
    slopes = tuple(2.0 ** (-8.0 * (h + 1) / DIFF_HEADS) for h in range(DIFF_HEADS))
    ff_tile = 512
    bound_margin = 1.02

    for l in range(depth):
        amax = lambda g: jnp.max(jnp.abs(g.astype(F32)))
        diff_bound = amax(diff_q_norm[l]) * amax(diff_k_norm[l]) * (DIFF_HALF ** 0.5 * bound_margin)
        gqa_bound = amax(gqa_q_norm[l]) * amax(gqa_k_norm[l]) * (HEAD_DIM ** 0.5 * bound_margin)
        diff_ok = diff_bound <= SHIFT_MAX
        diff_shift = jnp.where(diff_ok, diff_bound, 0.0)
        aq1, aq2, ak1, ak2, av, bq, bk, bv = _in_proj(
            xs, attn_norm[l], w_in[l].astype(BF16), tables, diff_shift,
            diff_q_norm[l], diff_k_norm[l], gqa_q_norm[l], gqa_k_norm[l], seq=seq, slopes=slopes)
        lam_init = 0.8 - 0.6 * math.exp(-0.3 * l)
        lp = diff_lambda[l].astype(F32)
        lam = jnp.exp(jnp.sum(lp[0] * lp[1])) - jnp.exp(jnp.sum(lp[2] * lp[3])) + lam_init
        ao = _diff_attention(aq1, aq2, ak1, ak2, av, lam, diff_subln[l], diff_ok,
                             batch=batch, seq=seq, out_scale=1.0 - lam_init)
        bo = _gqa_attention(bq, bk, bv, gqa_bound, batch=batch, seq=seq)
        xs = _out_proj(ao, bo, w_out[l].astype(BF16), xs)

        i = l // 2
        if l % 2 == 0:
            wg = _pad_ff(dense_w_gate[i].astype(BF16), 1, ff_tile)
            wu = _pad_ff(dense_w_up[i].astype(BF16), 1, ff_tile)
            wd = _pad_ff(dense_w_down[i].astype(BF16), 0, ff_tile)
            xs = _ffn_dense(xs, ffn_norm[l], wg, wu, wd, tf=ff_tile)
        else:
            xs = _moe(xs, ffn_norm[l], moe_router[i].astype(BF16), moe_w_gate[i].astype(BF16),
                      moe_w_up[i].astype(BF16), moe_w_down[i].astype(BF16))
    return xs.reshape(batch, seq, d)
```

```python
import functools
import math

import numpy as np
import jax
import jax.numpy as jnp
from jax import lax
from jax.experimental import pallas as pl
from jax.experimental.pallas import tpu as pltpu

HEAD_DIM = 128
DIFF_HALF = HEAD_DIM // 2
DIFF_HEADS = 8
GQA_HEADS = 8
GQA_KV_HEADS = 2
GQA_GROUP = GQA_HEADS // GQA_KV_HEADS
DIFF_WIDTH = DIFF_HEADS * HEAD_DIM
GQA_WIDTH = GQA_HEADS * HEAD_DIM
GQA_KV_WIDTH = GQA_KV_HEADS * HEAD_DIM
GRID_W = 64
ROPE_THETA = 10000.0
ROPE_AXIS_DIM = HEAD_DIM // 2
N_EXPERTS = 8
TOP_K = 2
EPS = 1e-6

LANES = 128
VMEM_LIMIT_BYTES = 52 * 1024 * 1024

BF16 = jnp.bfloat16
F32 = jnp.float32


def _params(*sem):
    return pltpu.CompilerParams(dimension_semantics=sem, vmem_limit_bytes=VMEM_LIMIT_BYTES)


def _rms_rows(xf, g):
    ms = jnp.mean(xf * xf, axis=-1, keepdims=True)
    return (xf * lax.rsqrt(ms + EPS)) * g


ATTN_TILE = 512
DIFF_QSUB = 4
GQA_QTILE = 1024
GQA_KTILE = 1024
AUG0 = DIFF_HALF
LOG2E = 1.4426950408889634


IN_PROJ_TN = 512


def _in_proj_kernel(shift_ref, x_ref, g_ref, w_ref, cos_ref, sin_ref, qaug_ref, kaug_ref,
                    dq_ref, dk_ref, gq_ref, gk_ref,
                    aq1_ref, aq2_ref, ak1_ref, ak2_ref, av_ref, bq_ref, bk_ref, bv_ref, n_ref, *, slopes):
    j = pl.program_id(1)

    @pl.when(j == 0)
    def _():
        n_ref[...] = _rms_rows(x_ref[...], g_ref[...]).astype(n_ref.dtype)

    acc = jnp.dot(n_ref[...], w_ref[...], preferred_element_type=F32)
    heads_per_tile = IN_PROJ_TN // HEAD_DIM
    chunk = lambda c: acc[:, c * HEAD_DIM:(c + 1) * HEAD_DIM]
    lanes = lambda c: slice(c * HEAD_DIM, (c + 1) * HEAD_DIM)

    lane = lax.broadcasted_iota(jnp.int32, (1, HEAD_DIM), 1)
    lo = lane < DIFF_HALF
    even = (lane & 1) == 0

    def half_rms(x, g):
        sq = x * x
        s_lo = jnp.sum(jnp.where(lo, sq, 0.0), axis=-1, keepdims=True)
        s_hi = jnp.sum(jnp.where(lo, 0.0, sq), axis=-1, keepdims=True)
        ms = jnp.where(lo, s_lo, s_hi) * (1.0 / DIFF_HALF)
        return (x * lax.rsqrt(ms + EPS)) * g

    def rope(y):
        sw = jnp.where(even, pltpu.roll(y, HEAD_DIM - 1, 1), pltpu.roll(y, 1, 1))
        return y * cos_ref[...] + sw * sin_ref[...]

    for jq in range(DIFF_HEADS // heads_per_tile):
        @pl.when(j == jq)
        def _():
            tail = jnp.where(lane == AUG0 + 4, 1.0, 0.0) + jnp.where(lane == AUG0 + 5, -shift_ref[0], 0.0)
            for c in range(heads_per_tile):
                q = half_rms(chunk(c), dq_ref[...]) * (DIFF_HALF ** -0.5)
                qa = qaug_ref[...] * slopes[jq * heads_per_tile + c] + tail
                aq1_ref[:, lanes(c)] = jnp.where(lo, q, qa).astype(BF16)
                aq2_ref[:, lanes(c)] = jnp.where(lo, pltpu.roll(q, DIFF_HALF, 1), qa).astype(BF16)

    @pl.when((j == 2) | (j == 3))
    def _():
        kaug = kaug_ref[...]
        for c in range(heads_per_tile):
            k = half_rms(chunk(c), dk_ref[...])
            ak1_ref[:, lanes(c)] = jnp.where(lo, k, kaug).astype(BF16)
            ak2_ref[:, lanes(c)] = jnp.where(lo, pltpu.roll(k, DIFF_HALF, 1), kaug).astype(BF16)

    @pl.when((j == 4) | (j == 5))
    def _():
        av_ref[...] = acc.astype(BF16)

    @pl.when((j == 6) | (j == 7))
    def _():
        for c in range(heads_per_tile):
            q = rope(_rms_rows(chunk(c), gq_ref[...])) * (HEAD_DIM ** -0.5 * LOG2E)
            bq_ref[:, lanes(c)] = q.astype(BF16)

    @pl.when(j == 8)
    def _():
        for c in range(GQA_KV_HEADS):
            bk_ref[:, lanes(c)] = rope(_rms_rows(chunk(c), gk_ref[...])).astype(BF16)
            bv_ref[:, lanes(c)] = chunk(GQA_KV_HEADS + c).astype(BF16)


def _in_proj(x, g, w, tables, shift, dq, dk, gq, gk, *, seq, slopes, tm=1024):
    n, d = x.shape
    tn = IN_PROJ_TN
    assert w.shape[1] == 3 * DIFF_WIDTH + GQA_WIDTH + 2 * GQA_KV_WIDTH and 2 * GQA_KV_WIDTH == tn
    tm = min(tm, seq)
    nt = seq // tm
    pos = lambda i, j: (i % nt, 0)
    fixed = lambda i, j: (0, 0)
    wide = lambda j0: (lambda i, j: (i, jnp.clip(j - j0, 0, 1)))
    dq2 = jnp.concatenate([dq, dq]).reshape(1, HEAD_DIM)
    dk2 = jnp.concatenate([dk, dk]).reshape(1, HEAD_DIM)
    kv = (n, GQA_KV_WIDTH)
    shapes = [(n, DIFF_WIDTH)] * 5 + [(n, GQA_WIDTH), kv, kv]
    starts = [0, 0, 2, 2, 4, 6]
    return pl.pallas_call(
        functools.partial(_in_proj_kernel, slopes=slopes),
        grid=(n // tm, w.shape[1] // tn),
        in_specs=[
            pl.BlockSpec(memory_space=pltpu.SMEM),
            pl.BlockSpec((tm, d), lambda i, j: (i, 0)),
            pl.BlockSpec((1, d), fixed),
            pl.BlockSpec((d, tn), lambda i, j: (0, j)),
        ] + [pl.BlockSpec((tm, HEAD_DIM), pos)] * 4 + [pl.BlockSpec((1, HEAD_DIM), fixed)] * 4,
        out_specs=[pl.BlockSpec((tm, tn), wide(j0)) for j0 in starts]
        + [pl.BlockSpec((tm, GQA_KV_WIDTH), lambda i, j: (i, 0))] * 2,
        out_shape=[jax.ShapeDtypeStruct(sh, BF16) for sh in shapes],
        scratch_shapes=[pltpu.VMEM((tm, d), BF16)],
        compiler_params=_params("parallel", "arbitrary"),
        name="in_proj_heads",
    )(shift.reshape(1), x, g.reshape(1, d), w, *tables, dq2, dk2, gq.reshape(1, HEAD_DIM), gk.reshape(1, HEAD_DIM))


def _position_tables(seq, tile):
    rows_n = seq // GRID_W
    row = jnp.repeat(jnp.arange(rows_n, dtype=F32), GRID_W, total_repeat_length=seq)
    col = jnp.tile(jnp.arange(GRID_W, dtype=F32), rows_n)
    inv = ROPE_THETA ** (-jnp.arange(0, ROPE_AXIS_DIM, 2, dtype=F32) / ROPE_AXIS_DIM)
    ang = jnp.concatenate([row[:, None] * inv, col[:, None] * inv], axis=-1)
    cos = jnp.repeat(jnp.cos(ang), 2, axis=-1)
    sin = jnp.sin(ang)
    sin_signed = jnp.stack([-sin, sin], axis=-1).reshape(seq, HEAD_DIM)
    loc = np.arange(seq) % tile
    hi = (loc // 16).astype(np.float32)
    lo = (loc % 16).astype(np.float32)
    qaug = np.zeros((seq, HEAD_DIM), np.float32)
    kaug = np.zeros((seq, HEAD_DIM), np.float32)
    qaug[:, AUG0 + 0] = 16.0
    qaug[:, AUG0 + 1] = 1.0
    qaug[:, AUG0 + 2] = -16.0 * hi
    qaug[:, AUG0 + 3] = -lo
    kaug[:, AUG0 + 0] = hi
    kaug[:, AUG0 + 1] = lo
    kaug[:, AUG0 + 2:AUG0 + 6] = 1.0
    return cos, sin_signed, jnp.asarray(qaug), jnp.asarray(kaug)


SHIFT_MAX = 40.0
UNDERFLOW = 106.0


def _qk(q, k):
    return lax.dot_general(q, k, (((1,), (1,)), ((), ())), preferred_element_type=F32)


def _lane_fold(p):
    out = p[:, :LANES]
    for c in range(1, p.shape[1] // LANES):
        out = out + p[:, c * LANES:(c + 1) * LANES]
    return out


def _online_update(s, v, m_ref, l_ref, acc_ref, idx, exp_fn):
    m_old = m_ref[idx]
    m_new = jnp.maximum(m_old, jnp.max(s, axis=-1, keepdims=True))
    alpha = exp_fn(m_old - m_new)
    p = exp_fn(s - m_new)
    l_ref[idx] = alpha * l_ref[idx] + jnp.sum(p, axis=-1, keepdims=True)
    acc_ref[idx] = alpha * acc_ref[idx] + jnp.dot(p.astype(BF16), v, preferred_element_type=F32)
    m_ref[idx] = m_new


def _shifted_update(p, v, l_ref, acc_ref, idx):
    l_ref[idx] += _lane_fold(p)
    acc_ref[idx] += jnp.dot(p.astype(BF16), v, preferred_element_type=F32)


def _row_total(l_ref, idx, stable):
    return l_ref[idx] if stable else jnp.sum(l_ref[idx], axis=-1, keepdims=True)


def _diff_attn_kernel(lam_ref, q1_ref, q2_ref, k1_ref, k2_ref, v_ref, g_ref, o_ref, *scratch,
                      slope, t, qsub, nk, window, out_scale, stable):
    if stable:
        m_ref, l_ref, acc_ref = scratch
        m_ref[...] = jnp.full(m_ref.shape, -jnp.inf, F32)
    else:
        l_ref, acc_ref = scratch
    l_ref[...] = jnp.zeros(l_ref.shape, F32)
    acc_ref[...] = jnp.zeros(acc_ref.shape, F32)
    q0 = qsub * pl.program_id(1)

    def tile(sub, d, ki, bias):
        rows = pl.ds(sub * t, t)
        keys = pl.ds(pl.multiple_of(ki * t, t), t)
        lane = lax.broadcasted_iota(jnp.int32, (1, HEAD_DIM), 1)
        sig = jnp.where(d > 0, 1.0, jnp.where(d < 0, -1.0, 0.0))
        tval = (-slope * t) * jnp.abs(d).astype(F32)
        mult = jnp.where(lane < AUG0, 1.0,
                         jnp.where(lane < AUG0 + 4, sig, jnp.where(lane == AUG0 + 4, tval, 1.0))).astype(BF16)
        v = v_ref[keys, :]
        for c, (q_ref, k_ref) in enumerate(((q1_ref, k1_ref), (q2_ref, k2_ref))):
            s = _qk(q_ref[rows, :] * mult, k_ref[keys, :])
            if bias is not None:
                s = s + bias
            if stable:
                _online_update(s, v, m_ref, l_ref, acc_ref, qsub * c + sub, jnp.exp)
            else:
                _shifted_update(jnp.exp(s), v, l_ref, acc_ref, qsub * c + sub)

    def key_tile(ki, carry):
        offs = [q0 + sub - ki for sub in range(qsub)]
        offdiag = [(d != 0) if window is None else (d != 0) & (jnp.abs(d) <= window) for d in offs]
        all_off = functools.reduce(jnp.logical_and, offdiag)

        @pl.when(all_off)
        def _():
            for sub, d in enumerate(offs):
                tile(sub, d, ki, None)

        for sub, d in enumerate(offs):
            @pl.when(offdiag[sub] & jnp.logical_not(all_off))
            def _():
                tile(sub, d, ki, None)

            @pl.when(d == 0)
            def _():
                r = lax.broadcasted_iota(jnp.int32, (t, t), 0)
                c = lax.broadcasted_iota(jnp.int32, (t, t), 1)
                tile(sub, d, ki, jnp.abs(r - c).astype(F32) * (-slope))
        return carry

    if window is None:
        lo, hi = 0, nk
    else:
        lo = jnp.maximum(q0 - window, 0)
        hi = jnp.minimum(q0 + qsub + window, nk)
    lax.fori_loop(lo, hi, key_tile, 0)

    for sub in range(qsub):
        o1 = acc_ref[sub] / _row_total(l_ref, sub, stable)
        o2 = acc_ref[qsub + sub] / _row_total(l_ref, qsub + sub, stable)
        o = _rms_rows(o1 - lam_ref[0] * o2, g_ref[...]) * out_scale
        o_ref[pl.ds(sub * t, t), :] = o.astype(o_ref.dtype)


def _diff_window(slope, tile, nk):
    w = int(math.floor((UNDERFLOW / slope - 1.0) / tile)) + 1
    return None if 2 * w + 1 >= nk else w


def _diff_attention_head(h, aq1, aq2, ak1, ak2, av, lam, subln, *, batch, seq, out_scale, stable):
    n = aq1.shape[0]
    t = min(ATTN_TILE, seq)
    nk = seq // t
    qsub = DIFF_QSUB if nk % DIFF_QSUB == 0 else 1
    nqb = nk // qsub
    slope = 2.0 ** (-8.0 * (h + 1) / DIFF_HEADS)
    window = None if stable else _diff_window(slope, t, nk)
    qmap = lambda b, qb: (b * nqb + qb, h)
    kmap = lambda b, qb: (b, h)
    kern = functools.partial(_diff_attn_kernel, slope=slope, t=t, qsub=qsub, nk=nk, window=window,
                             out_scale=out_scale, stable=stable)
    stat = [pltpu.VMEM((2 * qsub, t, 1), F32)] * 2 if stable else [pltpu.VMEM((2 * qsub, t, LANES), F32)]
    return pl.pallas_call(
        kern,
        grid=(batch, nqb),
        in_specs=[
            pl.BlockSpec(memory_space=pltpu.SMEM),
            pl.BlockSpec((qsub * t, HEAD_DIM), qmap),
            pl.BlockSpec((qsub * t, HEAD_DIM), qmap),
            pl.BlockSpec((seq, HEAD_DIM), kmap),
            pl.BlockSpec((seq, HEAD_DIM), kmap),
            pl.BlockSpec((seq, HEAD_DIM), kmap),
            pl.BlockSpec((1, HEAD_DIM), lambda b, qb: (0, 0)),
        ],
        out_specs=pl.BlockSpec((qsub * t, HEAD_DIM), lambda b, qb: (b * nqb + qb, 0)),
        out_shape=jax.ShapeDtypeStruct((n, HEAD_DIM), BF16),
        scratch_shapes=stat + [pltpu.VMEM((2 * qsub, t, HEAD_DIM), F32)],
        compiler_params=_params("parallel", "parallel"),
        name=f"diff_attention_h{h}" + ("_stable" if stable else ""),
    )(lam.reshape(1), aq1, aq2, ak1, ak2, av, subln.reshape(1, HEAD_DIM))


def _diff_attention(aq1, aq2, ak1, ak2, av, lam, subln, bound_ok, *, batch, seq, out_scale):
    def run(stable):
        def f(aq1, aq2, ak1, ak2, av, lam, subln):
            heads = [_diff_attention_head(h, aq1, aq2, ak1, ak2, av, lam, subln, batch=batch, seq=seq,
                                          out_scale=out_scale, stable=stable) for h in range(DIFF_HEADS)]
            return jnp.concatenate(heads, axis=-1)
        return f
    return lax.cond(bound_ok, run(False), run(True), aq1, aq2, ak1, ak2, av, lam, subln)


def _gqa_attn_kernel(shift_ref, q_ref, k_ref, v_ref, o_ref, *scratch, tk, nk, stable):
    if stable:
        m_ref, l_ref, acc_ref = scratch
        m_ref[...] = jnp.full(m_ref.shape, -jnp.inf, F32)
    else:
        l_ref, acc_ref = scratch
    l_ref[...] = jnp.zeros(l_ref.shape, F32)
    acc_ref[...] = jnp.zeros(acc_ref.shape, F32)

    def key_tile(ki, carry):
        keys = pl.ds(pl.multiple_of(ki * tk, tk), tk)
        k = k_ref[keys, :]
        v = v_ref[keys, :]
        for r in range(GQA_GROUP):
            s = _qk(q_ref[:, r * HEAD_DIM:(r + 1) * HEAD_DIM], k)
            if stable:
                _online_update(s, v, m_ref, l_ref, acc_ref, r, jnp.exp2)
            else:
                _shifted_update(jnp.exp2(s - shift_ref[0]), v, l_ref, acc_ref, r)
        return carry

    lax.fori_loop(0, nk, key_tile, 0)

    for r in range(GQA_GROUP):
        o = acc_ref[r] / _row_total(l_ref, r, stable)
        o_ref[:, r * HEAD_DIM:(r + 1) * HEAD_DIM] = o.astype(o_ref.dtype)


def _gqa_attention_call(shift, bq, bk, bv, *, batch, seq, stable):
    n = bq.shape[0]
    tk = min(GQA_KTILE, seq)
    tq = min(GQA_QTILE, seq)
    nq, nk = seq // tq, seq // tk
    gw = GQA_GROUP * HEAD_DIM
    qmap = lambda b, g, qi: (b * nq + qi, g)
    kmap = lambda b, g, qi: (b, g)
    stat = [pltpu.VMEM((GQA_GROUP, tq, 1), F32)] * 2 if stable else [pltpu.VMEM((GQA_GROUP, tq, LANES), F32)]
    return pl.pallas_call(
        functools.partial(_gqa_attn_kernel, tk=tk, nk=nk, stable=stable),
        grid=(batch, GQA_KV_HEADS, nq),
        in_specs=[
            pl.BlockSpec(memory_space=pltpu.SMEM),
            pl.BlockSpec((tq, gw), qmap),
            pl.BlockSpec((seq, HEAD_DIM), kmap),
            pl.BlockSpec((seq, HEAD_DIM), kmap),
        ],
        out_specs=pl.BlockSpec((tq, gw), qmap),
        out_shape=jax.ShapeDtypeStruct((n, GQA_WIDTH), BF16),
        scratch_shapes=stat + [pltpu.VMEM((GQA_GROUP, tq, HEAD_DIM), F32)],
        compiler_params=_params("parallel", "parallel", "parallel"),
        name="gqa_attention" + ("_stable" if stable else ""),
    )(shift.reshape(1), bq, bk, bv)


def _gqa_attention(bq, bk, bv, bound, *, batch, seq):
    shift = bound * LOG2E
    return lax.cond(
        bound <= SHIFT_MAX,
        functools.partial(_gqa_attention_call, batch=batch, seq=seq, stable=False),
        functools.partial(_gqa_attention_call, batch=batch, seq=seq, stable=True),
        shift, bq, bk, bv)


def _out_proj_kernel(a_ref, b_ref, wa_ref, wb_ref, x_ref, o_ref):
    acc = jnp.dot(a_ref[...], wa_ref[...], preferred_element_type=F32)
    acc = acc + jnp.dot(b_ref[...], wb_ref[...], preferred_element_type=F32)
    o_ref[...] = x_ref[...] + acc


def _out_proj(ao, bo, w_out, x, *, tm=1024, tn=512):
    n, d = x.shape
    tm = min(tm, n)
    tn = min(tn, d)
    wa, wb = ao.shape[1], bo.shape[1]
    return pl.pallas_call(
        _out_proj_kernel,
        grid=(n // tm, d // tn),
        in_specs=[
            pl.BlockSpec((tm, wa), lambda i, j: (i, 0)),
            pl.BlockSpec((tm, wb), lambda i, j: (i, 0)),
            pl.BlockSpec((wa, tn), lambda i, j: (0, j)),
            pl.BlockSpec((wb, tn), lambda i, j: (wa // wb, j)),
            pl.BlockSpec((tm, tn), lambda i, j: (i, j)),
        ],
        out_specs=pl.BlockSpec((tm, tn), lambda i, j: (i, j)),
        out_shape=jax.ShapeDtypeStruct((n, d), F32),
        compiler_params=_params("parallel", "parallel"),
        name="out_proj",
    )(ao, bo, w_out, w_out, x)


def _swiglu_act(g, u):
    return (g * jax.nn.sigmoid(g)) * u


def _ffn_kernel(x_ref, gn_ref, wg_ref, wu_ref, wd_ref, o_ref, h_ref):
    @pl.when(pl.program_id(1) == 0)
    def _():
        xf = x_ref[...]
        h_ref[...] = _rms_rows(xf, gn_ref[...]).astype(h_ref.dtype)
        o_ref[...] = xf

    h = h_ref[...]
    g = jnp.dot(h, wg_ref[...], preferred_element_type=F32)
    u = jnp.dot(h, wu_ref[...], preferred_element_type=F32)
    a = _swiglu_act(g, u).astype(BF16)
    o_ref[...] += jnp.dot(a, wd_ref[...], preferred_element_type=F32)


def _ffn_dense(x, gn, wg, wu, wd, *, tm=512, tf=512):
    n, d = x.shape
    f = wg.shape[1]
    tm = min(tm, n)
    return pl.pallas_call(
        _ffn_kernel,
        grid=(n // tm, f // tf),
        in_specs=[
            pl.BlockSpec((tm, d), lambda i, j: (i, 0)),
            pl.BlockSpec((1, d), lambda i, j: (0, 0)),
            pl.BlockSpec((d, tf), lambda i, j: (0, j)),
            pl.BlockSpec((d, tf), lambda i, j: (0, j)),
            pl.BlockSpec((tf, d), lambda i, j: (j, 0)),
        ],
        out_specs=pl.BlockSpec((tm, d), lambda i, j: (i, 0)),
        out_shape=jax.ShapeDtypeStruct((n, d), F32),
        scratch_shapes=[pltpu.VMEM((tm, d), BF16)],
        compiler_params=_params("parallel", "arbitrary"),
        name="ffn_dense",
    )(x, gn.reshape(1, d), wg, wu, wd)


def _pad_ff(w, axis, mult):
    f = w.shape[axis]
    fp = -(-f // mult) * mult
    if fp == f:
        return w
    pad = [(0, 0)] * w.ndim
    pad[axis] = (0, fp - f)
    return jnp.pad(w, pad)


def _router_kernel(x_ref, gn_ref, r_ref, h_ref, idx_ref, gate_ref):
    h = _rms_rows(x_ref[...], gn_ref[...])
    h_ref[...] = h
    logits = jnp.dot(h.astype(BF16), r_ref[...], preferred_element_type=F32)
    lane = lax.broadcasted_iota(jnp.int32, logits.shape, 1)
    m1 = jnp.max(logits, axis=-1, keepdims=True)
    i1 = jnp.min(jnp.where(logits == m1, lane, N_EXPERTS), axis=-1, keepdims=True)
    rest = jnp.where(lane == i1, -jnp.inf, logits)
    m2 = jnp.max(rest, axis=-1, keepdims=True)
    i2 = jnp.min(jnp.where(rest == m2, lane, N_EXPERTS), axis=-1, keepdims=True)
    e2 = jnp.exp(m2 - m1)
    den = 1.0 + e2
    first = lax.broadcasted_iota(jnp.int32, idx_ref.shape, 1) == 0
    idx_ref[...] = jnp.where(first, i1, i2)
    gate_ref[...] = jnp.where(first, 1.0 / den, e2 / den)


def _router(x, gn, router, *, tm=512):
    n, d = x.shape
    tm = min(tm, n)
    return pl.pallas_call(
        _router_kernel,
        grid=(n // tm,),
        in_specs=[
            pl.BlockSpec((tm, d), lambda i: (i, 0)),
            pl.BlockSpec((1, d), lambda i: (0, 0)),
            pl.BlockSpec((d, N_EXPERTS), lambda i: (0, 0)),
        ],
        out_specs=[
            pl.BlockSpec((tm, d), lambda i: (i, 0)),
            pl.BlockSpec((tm, TOP_K), lambda i: (i, 0)),
            pl.BlockSpec((tm, TOP_K), lambda i: (i, 0)),
        ],
        out_shape=[
            jax.ShapeDtypeStruct((n, d), F32),
            jax.ShapeDtypeStruct((n, TOP_K), jnp.int32),
            jax.ShapeDtypeStruct((n, TOP_K), F32),
        ],
        compiler_params=_params("parallel"),
        name="moe_router",
    )(x, gn.reshape(1, d), router)


def _gather_rows(idx_ref, src_ref, dst_ref, sem, rows):
    def start(r, carry):
        pltpu.make_async_copy(src_ref.at[pl.ds(idx_ref[r], 1)], dst_ref.at[pl.ds(r, 1)], sem).start()
        return carry

    lax.fori_loop(0, rows, start, 0, unroll=8)
    pltpu.make_async_copy(src_ref.at[pl.ds(0, rows)], dst_ref.at[pl.ds(0, rows)], sem).wait()


def _gather_kernel(idx_ref, src_ref, o_ref, buf_ref, sem, *, rows):
    _gather_rows(idx_ref, src_ref, buf_ref, sem, rows)
    o_ref[...] = buf_ref[...].astype(o_ref.dtype)


def _gather(src, idx, *, rows, out_dtype):
    r = idx.shape[0]
    d = src.shape[1]
    return pl.pallas_call(
        functools.partial(_gather_kernel, rows=rows),
        grid=(r // rows,),
        in_specs=[
            pl.BlockSpec((rows,), lambda i: (i,), memory_space=pltpu.SMEM),
            pl.BlockSpec(memory_space=pl.ANY),
        ],
        out_specs=pl.BlockSpec((rows, d), lambda i: (i, 0)),
        out_shape=jax.ShapeDtypeStruct((r, d), out_dtype),
        scratch_shapes=[pltpu.VMEM((rows, d), src.dtype), pltpu.SemaphoreType.DMA],
        compiler_params=_params("arbitrary"),
        name="moe_gather",
    )(idx, src)


def _moe_ffn_kernel(te_ref, nu_ref, h_ref, wg_ref, wu_ref, wd_ref, o_ref):
    i = pl.program_id(0)
    j = pl.program_id(1)

    @pl.when(j == 0)
    def _():
        o_ref[...] = jnp.zeros(o_ref.shape, F32)

    @pl.when(i < nu_ref[0])
    def _():
        h = h_ref[...]
        g = jnp.dot(h, wg_ref[...], preferred_element_type=F32)
        u = jnp.dot(h, wu_ref[...], preferred_element_type=F32)
        a = _swiglu_act(g, u).astype(BF16)
        o_ref[...] += jnp.dot(a, wd_ref[...], preferred_element_type=F32)


def _moe_ffn(tile_expert, n_used, h_sorted, wg, wu, wd, *, tm, tf=1024):
    r, d = h_sorted.shape
    f = wg.shape[2]
    nf = f // tf
    jj = lambda i, j, te, nu: jnp.where(i < nu[0], j, nf - 1)
    return pl.pallas_call(
        _moe_ffn_kernel,
        grid_spec=pltpu.PrefetchScalarGridSpec(
            num_scalar_prefetch=2,
            grid=(r // tm, nf),
            in_specs=[
                pl.BlockSpec((tm, d), lambda i, j, te, nu: (i, 0)),
                pl.BlockSpec((None, d, tf), lambda i, j, te, nu: (te[i], 0, jj(i, j, te, nu))),
                pl.BlockSpec((None, d, tf), lambda i, j, te, nu: (te[i], 0, jj(i, j, te, nu))),
                pl.BlockSpec((None, tf, d), lambda i, j, te, nu: (te[i], jj(i, j, te, nu), 0)),
            ],
            out_specs=pl.BlockSpec((tm, d), lambda i, j, te, nu: (i, 0)),
        ),
        out_shape=jax.ShapeDtypeStruct((r, d), F32),
        compiler_params=_params("arbitrary", "arbitrary"),
        name="moe_experts",
    )(tile_expert, n_used, h_sorted, wg, wu, wd)


def _combine_kernel(idx_ref, y_ref, x_ref, gate_ref, o_ref, buf_ref, sem, *, tc):
    _gather_rows(idx_ref, y_ref, buf_ref, sem, TOP_K * tc)
    gate = gate_ref[...]
    o_ref[...] = (x_ref[...] + gate[:, 0:1] * buf_ref[pl.ds(0, tc), :]
                  + gate[:, 1:2] * buf_ref[pl.ds(tc, tc), :])


def _combine(y_sorted, pos, gate, x, *, tc=256):
    n, d = x.shape
    tc = min(tc, n)
    idx = pos.reshape(n // tc, tc, TOP_K).transpose(0, 2, 1).reshape(-1)
    return pl.pallas_call(
        functools.partial(_combine_kernel, tc=tc),
        grid=(n // tc,),
        in_specs=[
            pl.BlockSpec((TOP_K * tc,), lambda i: (i,), memory_space=pltpu.SMEM),
            pl.BlockSpec(memory_space=pl.ANY),
            pl.BlockSpec((tc, d), lambda i: (i, 0)),
            pl.BlockSpec((tc, TOP_K), lambda i: (i, 0)),
        ],
        out_specs=pl.BlockSpec((tc, d), lambda i: (i, 0)),
        out_shape=jax.ShapeDtypeStruct((n, d), F32),
        scratch_shapes=[pltpu.VMEM((TOP_K * tc, d), F32), pltpu.SemaphoreType.DMA],
        compiler_params=_params("arbitrary"),
        name="moe_combine",
    )(idx, y_sorted, x, gate)


def _route_plan(idx, tm):
    n = idx.shape[0]
    a = n * TOP_K
    ntiles = a // tm + N_EXPERTS
    experts = jnp.arange(N_EXPERTS, dtype=jnp.int32)
    e_flat = idx.reshape(a)
    onehot = e_flat[:, None] == experts[None, :]
    order = jnp.argsort(e_flat, stable=True).astype(jnp.int32)
    counts = jnp.sum(onehot, axis=0).astype(jnp.int32)
    padded = ((counts + tm - 1) // tm) * tm
    pad_end = jnp.cumsum(padded)
    pad_start = pad_end - padded
    raw_start = jnp.cumsum(counts) - counts
    tile_start = jnp.arange(ntiles, dtype=jnp.int32) * tm
    tile_expert = jnp.minimum(jnp.sum(tile_start[:, None] >= pad_end[None, :], axis=1), N_EXPERTS - 1).astype(jnp.int32)
    rank = (tile_start - pad_start[tile_expert])[:, None] + jnp.arange(tm, dtype=jnp.int32)[None, :]
    valid = rank < counts[tile_expert][:, None]
    src = jnp.clip(raw_start[tile_expert][:, None] + rank, 0, a - 1)
    row_token = jnp.where(valid, order[src] // TOP_K, 0).reshape(ntiles * tm)
    sorted_pos = jnp.argsort(order).astype(jnp.int32)
    shift = jnp.sum(jnp.where(onehot, (pad_start - raw_start)[None, :], 0), axis=1)
    pos = sorted_pos + shift
    n_used = (pad_end[-1] // tm).reshape(1).astype(jnp.int32)
    return row_token, pos.reshape(n, TOP_K), tile_expert, n_used


def _moe(x, gn, router, wg, wu, wd, *, tm=512):
    n = x.shape[0]
    tm = min(tm, n)
    h, idx, gate = _router(x, gn, router)
    row_token, pos, tile_expert, n_used = _route_plan(idx, tm)
    h_sorted = _gather(h, row_token, rows=tm, out_dtype=BF16)
    y_sorted = _moe_ffn(tile_expert, n_used, h_sorted, wg, wu, wd, tm=tm)
    return _combine(y_sorted, pos, gate, x)


def kernel(x, attn_norm, w_in, diff_q_norm, diff_k_norm, diff_lambda, diff_subln, gqa_q_norm, gqa_k_norm, w_out, ffn_norm, dense_w_gate, dense_w_up, dense_w_down, moe_router, moe_w_gate, moe_w_up, moe_w_down):
    batch, seq, d = x.shape
    depth = w_in.shape[0]
    n = batch * seq
    xs = x.reshape(n, d).astype(F32)
    tables = _position_tables(seq, min(ATTN_TILE, seq))
    slopes = tuple(2.0 ** (-8.0 * (h + 1) / DIFF_HEADS) for h in range(DIFF_HEADS))
    ff_tile = 512
    bound_margin = 1.02

    for l in range(depth):
        amax = lambda g: jnp.max(jnp.abs(g.astype(F32)))
        diff_bound = amax(diff_q_norm[l]) * amax(diff_k_norm[l]) * (DIFF_HALF ** 0.5 * bound_margin)
        gqa_bound = amax(gqa_q_norm[l]) * amax(gqa_k_norm[l]) * (HEAD_DIM ** 0.5 * bound_margin)
        diff_ok = diff_bound <= SHIFT_MAX
        diff_shift = jnp.where(diff_ok, diff_bound, 0.0)
        aq1, aq2, ak1, ak2, av, bq, bk, bv = _in_proj(
            xs, attn_norm[l], w_in[l].astype(BF16), tables, diff_shift,
            diff_q_norm[l], diff_k_norm[l], gqa_q_norm[l], gqa_k_norm[l], seq=seq, slopes=slopes)
        lam_init = 0.8 - 0.6 * math.exp(-0.3 * l)
        lp = diff_lambda[l].astype(F32)
        lam = jnp.exp(jnp.sum(lp[0] * lp[1])) - jnp.exp(jnp.sum(lp[2] * lp[3])) + lam_init
        ao = _diff_attention(aq1, aq2, ak1, ak2, av, lam, diff_subln[l], diff_ok,
                             batch=batch, seq=seq, out_scale=1.0 - lam_init)
        bo = _gqa_attention(bq, bk, bv, gqa_bound, batch=batch, seq=seq)
        xs = _out_proj(ao, bo, w_out[l].astype(BF16), xs)

        i = l // 2
        if l % 2 == 0:
            wg = _pad_ff(dense_w_gate[i].astype(BF16), 1, ff_tile)
            wu = _pad_ff(dense_w_up[i].astype(BF16), 1, ff_tile)
            wd = _pad_ff(dense_w_down[i].astype(BF16), 0, ff_tile)
            xs = _ffn_dense(xs, ffn_norm[l], wg, wu, wd, tf=ff_tile)
        else:
            xs = _moe(xs, ffn_norm[l], moe_router[i].astype(BF16), moe_w_gate[i].astype(BF16),
                      moe_w_up[i].astype(BF16), moe_w_down[i].astype(BF16))
    return xs.reshape(batch, seq, d)
```

```python
import functools
import math

import numpy as np
import jax
import jax.numpy as jnp
from jax import lax
from jax.experimental import pallas as pl
from jax.experimental.pallas import tpu as pltpu

HEAD_DIM = 128
DIFF_HALF = HEAD_DIM // 2
DIFF_HEADS = 8
GQA_HEADS = 8
GQA_KV_HEADS = 2
GQA_GROUP = GQA_HEADS // GQA_KV_HEADS
DIFF_WIDTH = DIFF_HEADS * HEAD_DIM
GQA_WIDTH = GQA_HEADS * HEAD_DIM
GQA_KV_WIDTH = GQA_KV_HEADS * HEAD_DIM
GRID_W = 64
ROPE_THETA = 10000.0
ROPE_AXIS_DIM = HEAD_DIM // 2
N_EXPERTS = 8
TOP_K = 2
EPS = 1e-6

LANES = 128
VMEM_LIMIT_BYTES = 52 * 1024 * 1024

BF16 = jnp.bfloat16
F32 = jnp.float32


def _params(*sem):
    return pltpu.CompilerParams(dimension_semantics=sem, vmem_limit_bytes=VMEM_LIMIT_BYTES)


def _rms_rows(xf, g):
    ms = jnp.mean(xf * xf, axis=-1, keepdims=True)
    return (xf * lax.rsqrt(ms + EPS)) * g


ATTN_TILE = 512
DIFF_QSUB = 4
GQA_QTILE = 1024
GQA_KTILE = 1024
AUG0 = DIFF_HALF
LOG2E = 1.4426950408889634


IN_PROJ_TN = 512


def _in_proj_kernel(shift_ref, x_ref, g_ref, w_ref, cos_ref, sin_ref, qaug_ref, kaug_ref,
                    dq_ref, dk_ref, gq_ref, gk_ref,
                    aq1_ref, aq2_ref, ak1_ref, ak2_ref, av_ref, bq_ref, bk_ref, bv_ref, n_ref, *, slopes):
    j = pl.program_id(1)

    @pl.when(j == 0)
    def _():
        n_ref[...] = _rms_rows(x_ref[...], g_ref[...]).astype(n_ref.dtype)

    acc = jnp.dot(n_ref[...], w_ref[...], preferred_element_type=F32)
    heads_per_tile = IN_PROJ_TN // HEAD_DIM
    chunk = lambda c: acc[:, c * HEAD_DIM:(c + 1) * HEAD_DIM]
    lanes = lambda c: slice(c * HEAD_DIM, (c + 1) * HEAD_DIM)

    lane = lax.broadcasted_iota(jnp.int32, (1, HEAD_DIM), 1)
    lo = lane < DIFF_HALF
    even = (lane & 1) == 0

    def half_rms(x, g):
        sq = x * x
        s_lo = jnp.sum(jnp.where(lo, sq, 0.0), axis=-1, keepdims=True)
        s_hi = jnp.sum(jnp.where(lo, 0.0, sq), axis=-1, keepdims=True)
        ms = jnp.where(lo, s_lo, s_hi) * (1.0 / DIFF_HALF)
        return (x * lax.rsqrt(ms + EPS)) * g

    def rope(y):
        sw = jnp.where(even, pltpu.roll(y, HEAD_DIM - 1, 1), pltpu.roll(y, 1, 1))
        return y * cos_ref[...] + sw * sin_ref[...]

    for jq in range(DIFF_HEADS // heads_per_tile):
        @pl.when(j == jq)
        def _():
            tail = jnp.where(lane == AUG0 + 4, 1.0, 0.0) + jnp.where(lane == AUG0 + 5, -shift_ref[0], 0.0)
            for c in range(heads_per_tile):
                q = half_rms(chunk(c), dq_ref[...]) * (DIFF_HALF ** -0.5)
                qa = qaug_ref[...] * slopes[jq * heads_per_tile + c] + tail
                aq1_ref[:, lanes(c)] = jnp.where(lo, q, qa).astype(BF16)
                aq2_ref[:, lanes(c)] = jnp.where(lo, pltpu.roll(q, DIFF_HALF, 1), qa).astype(BF16)

    @pl.when((j == 2) | (j == 3))
    def _():
        kaug = kaug_ref[...]
        for c in range(heads_per_tile):
            k = half_rms(chunk(c), dk_ref[...])
            ak1_ref[:, lanes(c)] = jnp.where(lo, k, kaug).astype(BF16)
            ak2_ref[:, lanes(c)] = jnp.where(lo, pltpu.roll(k, DIFF_HALF, 1), kaug).astype(BF16)

    @pl.when((j == 4) | (j == 5))
    def _():
        av_ref[...] = acc.astype(BF16)

    @pl.when((j == 6) | (j == 7))
    def _():
        for c in range(heads_per_tile):
            q = rope(_rms_rows(chunk(c), gq_ref[...])) * (HEAD_DIM ** -0.5 * LOG2E)
            bq_ref[:, lanes(c)] = q.astype(BF16)

    @pl.when(j == 8)
    def _():
        for c in range(GQA_KV_HEADS):
            bk_ref[:, lanes(c)] = rope(_rms_rows(chunk(c), gk_ref[...])).astype(BF16)
            bv_ref[:, lanes(c)] = chunk(GQA_KV_HEADS + c).astype(BF16)


def _in_proj(x, g, w, tables, shift, dq, dk, gq, gk, *, seq, slopes, tm=1024):
    n, d = x.shape
    tn = IN_PROJ_TN
    assert w.shape[1] == 3 * DIFF_WIDTH + GQA_WIDTH + 2 * GQA_KV_WIDTH and 2 * GQA_KV_WIDTH == tn
    tm = min(tm, seq)
    nt = seq // tm
    pos = lambda i, j: (i % nt, 0)
    fixed = lambda i, j: (0, 0)
    wide = lambda j0: (lambda i, j: (i, jnp.clip(j - j0, 0, 1)))
    dq2 = jnp.concatenate([dq, dq]).reshape(1, HEAD_DIM)
    dk2 = jnp.concatenate([dk, dk]).reshape(1, HEAD_DIM)
    kv = (n, GQA_KV_WIDTH)
    shapes = [(n, DIFF_WIDTH)] * 5 + [(n, GQA_WIDTH), kv, kv]
    starts = [0, 0, 2, 2, 4, 6]
    return pl.pallas_call(
        functools.partial(_in_proj_kernel, slopes=slopes),
        grid=(n // tm, w.shape[1] // tn),
        in_specs=[
            pl.BlockSpec(memory_space=pltpu.SMEM),
            pl.BlockSpec((tm, d), lambda i, j: (i, 0)),
            pl.BlockSpec((1, d), fixed),
            pl.BlockSpec((d, tn), lambda i, j: (0, j)),
        ] + [pl.BlockSpec((tm, HEAD_DIM), pos)] * 4 + [pl.BlockSpec((1, HEAD_DIM), fixed)] * 4,
        out_specs=[pl.BlockSpec((tm, tn), wide(j0)) for j0 in starts]
        + [pl.BlockSpec((tm, GQA_KV_WIDTH), lambda i, j: (i, 0))] * 2,
        out_shape=[jax.ShapeDtypeStruct(sh, BF16) for sh in shapes],
        scratch_shapes=[pltpu.VMEM((tm, d), BF16)],
        compiler_params=_params("parallel", "arbitrary"),
        name="in_proj_heads",
    )(shift.reshape(1), x, g.reshape(1, d), w, *tables, dq2, dk2, gq.reshape(1, HEAD_DIM), gk.reshape(1, HEAD_DIM))


def _position_tables(seq, tile):
    rows_n = seq // GRID_W
    row = jnp.repeat(jnp.arange(rows_n, dtype=F32), GRID_W, total_repeat_length=seq)
    col = jnp.tile(jnp.arange(GRID_W, dtype=F32), rows_n)
    inv = ROPE_THETA ** (-jnp.arange(0, ROPE_AXIS_DIM, 2, dtype=F32) / ROPE_AXIS_DIM)
    ang = jnp.concatenate([row[:, None] * inv, col[:, None] * inv], axis=-1)
    cos = jnp.repeat(jnp.cos(ang), 2, axis=-1)
    sin = jnp.sin(ang)
    sin_signed = jnp.stack([-sin, sin], axis=-1).reshape(seq, HEAD_DIM)
    loc = np.arange(seq) % tile
    hi = (loc // 16).astype(np.float32)
    lo = (loc % 16).astype(np.float32)
    qaug = np.zeros((seq, HEAD_DIM), np.float32)
    kaug = np.zeros((seq, HEAD_DIM), np.float32)
    qaug[:, AUG0 + 0] = 16.0
    qaug[:, AUG0 + 1] = 1.0
    qaug[:, AUG0 + 2] = -16.0 * hi
    qaug[:, AUG0 + 3] = -lo
    kaug[:, AUG0 + 0] = hi
    kaug[:, AUG0 + 1] = lo
    kaug[:, AUG0 + 2:AUG0 + 6] = 1.0
    return cos, sin_signed, jnp.asarray(qaug), jnp.asarray(kaug)


SHIFT_MAX = 40.0
UNDERFLOW = 106.0


def _qk(q, k):
    return lax.dot_general(q, k, (((1,), (1,)), ((), ())), preferred_element_type=F32)


def _lane_fold(p):
    out = p[:, :LANES]
    for c in range(1, p.shape[1] // LANES):
        out = out + p[:, c * LANES:(c + 1) * LANES]
    return out


def _online_update(s, v, m_ref, l_ref, acc_ref, idx, exp_fn):
    m_old = m_ref[idx]
    m_new = jnp.maximum(m_old, jnp.max(s, axis=-1, keepdims=True))
    alpha = exp_fn(m_old - m_new)
    p = exp_fn(s - m_new)
    l_ref[idx] = alpha * l_ref[idx] + jnp.sum(p, axis=-1, keepdims=True)
    acc_ref[idx] = alpha * acc_ref[idx] + jnp.dot(p.astype(BF16), v, preferred_element_type=F32)
    m_ref[idx] = m_new


def _shifted_update(p, v, l_ref, acc_ref, idx):
    l_ref[idx] += _lane_fold(p)
    acc_ref[idx] += jnp.dot(p.astype(BF16), v, preferred_element_type=F32)


def _row_total(l_ref, idx, stable):
    return l_ref[idx] if stable else jnp.sum(l_ref[idx], axis=-1, keepdims=True)


def _diff_attn_kernel(lam_ref, q1_ref, q2_ref, k1_ref, k2_ref, v_ref, g_ref, o_ref, *scratch,
                      slope, t, qsub, nk, window, out_scale, stable):
    if stable:
        m_ref, l_ref, acc_ref = scratch
        m_ref[...] = jnp.full(m_ref.shape, -jnp.inf, F32)
    else:
        l_ref, acc_ref = scratch
    l_ref[...] = jnp.zeros(l_ref.shape, F32)
    acc_ref[...] = jnp.zeros(acc_ref.shape, F32)
    q0 = qsub * pl.program_id(1)

    def tile(sub, d, ki, bias):
        rows = pl.ds(sub * t, t)
        keys = pl.ds(pl.multiple_of(ki * t, t), t)
        lane = lax.broadcasted_iota(jnp.int32, (1, HEAD_DIM), 1)
        sig = jnp.where(d > 0, 1.0, jnp.where(d < 0, -1.0, 0.0))
        tval = (-slope * t) * jnp.abs(d).astype(F32)
        mult = jnp.where(lane < AUG0, 1.0,
                         jnp.where(lane < AUG0 + 4, sig, jnp.where(lane == AUG0 + 4, tval, 1.0))).astype(BF16)
        v = v_ref[keys, :]
        for c, (q_ref, k_ref) in enumerate(((q1_ref, k1_ref), (q2_ref, k2_ref))):
            s = _qk(q_ref[rows, :] * mult, k_ref[keys, :])
            if bias is not None:
                s = s + bias
            if stable:
                _online_update(s, v, m_ref, l_ref, acc_ref, qsub * c + sub, jnp.exp)
            else:
                _shifted_update(jnp.exp(s), v, l_ref, acc_ref, qsub * c + sub)

    def key_tile(ki, carry):
        offs = [q0 + sub - ki for sub in range(qsub)]
        offdiag = [(d != 0) if window is None else (d != 0) & (jnp.abs(d) <= window) for d in offs]
        all_off = functools.reduce(jnp.logical_and, offdiag)

        @pl.when(all_off)
        def _():
            for sub, d in enumerate(offs):
                tile(sub, d, ki, None)

        for sub, d in enumerate(offs):
            @pl.when(offdiag[sub] & jnp.logical_not(all_off))
            def _():
                tile(sub, d, ki, None)

            @pl.when(d == 0)
            def _():
                r = lax.broadcasted_iota(jnp.int32, (t, t), 0)
                c = lax.broadcasted_iota(jnp.int32, (t, t), 1)
                tile(sub, d, ki, jnp.abs(r - c).astype(F32) * (-slope))
        return carry

    if window is None:
        lo, hi = 0, nk
    else:
        lo = jnp.maximum(q0 - window, 0)
        hi = jnp.minimum(q0 + qsub + window, nk)
    lax.fori_loop(lo, hi, key_tile, 0)

    for sub in range(qsub):
        o1 = acc_ref[sub] / _row_total(l_ref, sub, stable)
        o2 = acc_ref[qsub + sub] / _row_total(l_ref, qsub + sub, stable)
        o = _rms_rows(o1 - lam_ref[0] * o2, g_ref[...]) * out_scale
        o_ref[pl.ds(sub * t, t), :] = o.astype(o_ref.dtype)


def _diff_window(slope, tile, nk):
    w = int(math.floor((UNDERFLOW / slope - 1.0) / tile)) + 1
    return None if 2 * w + 1 >= nk else w


def _diff_attention_head(h, aq1, aq2, ak1, ak2, av, lam, subln, *, batch, seq, out_scale, stable):
    n = aq1.shape[0]
    t = min(ATTN_TILE, seq)
    nk = seq // t
    qsub = DIFF_QSUB if nk % DIFF_QSUB == 0 else 1
    nqb = nk // qsub
    slope = 2.0 ** (-8.0 * (h + 1) / DIFF_HEADS)
    window = None if stable else _diff_window(slope, t, nk)
    qmap = lambda b, qb: (b * nqb + qb, h)
    kmap = lambda b, qb: (b, h)
    kern = functools.partial(_diff_attn_kernel, slope=slope, t=t, qsub=qsub, nk=nk, window=window,
                             out_scale=out_scale, stable=stable)
    stat = [pltpu.VMEM((2 * qsub, t, 1), F32)] * 2 if stable else [pltpu.VMEM((2 * qsub, t, LANES), F32)]
    return pl.pallas_call(
        kern,
        grid=(batch, nqb),
        in_specs=[
            pl.BlockSpec(memory_space=pltpu.SMEM),
            pl.BlockSpec((qsub * t, HEAD_DIM), qmap),
            pl.BlockSpec((qsub * t, HEAD_DIM), qmap),
            pl.BlockSpec((seq, HEAD_DIM), kmap),
            pl.BlockSpec((seq, HEAD_DIM), kmap),
            pl.BlockSpec((seq, HEAD_DIM), kmap),
            pl.BlockSpec((1, HEAD_DIM), lambda b, qb: (0, 0)),
        ],
        out_specs=pl.BlockSpec((qsub * t, HEAD_DIM), lambda b, qb: (b * nqb + qb, 0)),
        out_shape=jax.ShapeDtypeStruct((n, HEAD_DIM), BF16),
        scratch_shapes=stat + [pltpu.VMEM((2 * qsub, t, HEAD_DIM), F32)],
        compiler_params=_params("parallel", "parallel"),
        name=f"diff_attention_h{h}" + ("_stable" if stable else ""),
    )(lam.reshape(1), aq1, aq2, ak1, ak2, av, subln.reshape(1, HEAD_DIM))


def _diff_attention(aq1, aq2, ak1, ak2, av, lam, subln, bound_ok, *, batch, seq, out_scale):
    def run(stable):
        def f(aq1, aq2, ak1, ak2, av, lam, subln):
            heads = [_diff_attention_head(h, aq1, aq2, ak1, ak2, av, lam, subln, batch=batch, seq=seq,
                                          out_scale=out_scale, stable=stable) for h in range(DIFF_HEADS)]
            return jnp.concatenate(heads, axis=-1)
        return f
    return lax.cond(bound_ok, run(False), run(True), aq1, aq2, ak1, ak2, av, lam, subln)


def _gqa_attn_kernel(shift_ref, q_ref, k_ref, v_ref, o_ref, *scratch, tk, nk, stable):
    if stable:
        m_ref, l_ref, acc_ref = scratch
        m_ref[...] = jnp.full(m_ref.shape, -jnp.inf, F32)
    else:
        l_ref, acc_ref = scratch
    l_ref[...] = jnp.zeros(l_ref.shape, F32)
    acc_ref[...] = jnp.zeros(acc_ref.shape, F32)

    def key_tile(ki, carry):
        keys = pl.ds(pl.multiple_of(ki * tk, tk), tk)
        k = k_ref[keys, :]
        v = v_ref[keys, :]
        for r in range(GQA_GROUP):
            s = _qk(q_ref[:, r * HEAD_DIM:(r + 1) * HEAD_DIM], k)
            if stable:
                _online_update(s, v, m_ref, l_ref, acc_ref, r, jnp.exp2)
            else:
                _shifted_update(jnp.exp2(s - shift_ref[0]), v, l_ref, acc_ref, r)
        return carry

    lax.fori_loop(0, nk, key_tile, 0)

    for r in range(GQA_GROUP):
        o = acc_ref[r] / _row_total(l_ref, r, stable)
        o_ref[:, r * HEAD_DIM:(r + 1) * HEAD_DIM] = o.astype(o_ref.dtype)


def _gqa_attention_call(shift, bq, bk, bv, *, batch, seq, stable):
    n = bq.shape[0]
    tk = min(GQA_KTILE, seq)
    tq = min(GQA_QTILE, seq)
    nq, nk = seq // tq, seq // tk
    gw = GQA_GROUP * HEAD_DIM
    qmap = lambda b, g, qi: (b * nq + qi, g)
    kmap = lambda b, g, qi: (b, g)
    stat = [pltpu.VMEM((GQA_GROUP, tq, 1), F32)] * 2 if stable else [pltpu.VMEM((GQA_GROUP, tq, LANES), F32)]
    return pl.pallas_call(
        functools.partial(_gqa_attn_kernel, tk=tk, nk=nk, stable=stable),
        grid=(batch, GQA_KV_HEADS, nq),
        in_specs=[
            pl.BlockSpec(memory_space=pltpu.SMEM),
            pl.BlockSpec((tq, gw), qmap),
            pl.BlockSpec((seq, HEAD_DIM), kmap),
            pl.BlockSpec((seq, HEAD_DIM), kmap),
        ],
        out_specs=pl.BlockSpec((tq, gw), qmap),
        out_shape=jax.ShapeDtypeStruct((n, GQA_WIDTH), BF16),
        scratch_shapes=stat + [pltpu.VMEM((GQA_GROUP, tq, HEAD_DIM), F32)],
        compiler_params=_params("parallel", "parallel", "parallel"),
        name="gqa_attention" + ("_stable" if stable else ""),
    )(shift.reshape(1), bq, bk, bv)


def _gqa_attention(bq, bk, bv, bound, *, batch, seq):
    shift = bound * LOG2E
    return lax.cond(
        bound <= SHIFT_MAX,
        functools.partial(_gqa_attention_call, batch=batch, seq=seq, stable=False),
        functools.partial(_gqa_attention_call, batch=batch, seq=seq, stable=True),
        shift, bq, bk, bv)


def _out_proj_kernel(a_ref, b_ref, wa_ref, wb_ref, x_ref, o_ref):
    acc = jnp.dot(a_ref[...], wa_ref[...], preferred_element_type=F32)
    acc = acc + jnp.dot(b_ref[...], wb_ref[...], preferred_element_type=F32)
    o_ref[...] = x_ref[...] + acc


def _out_proj(ao, bo, w_out, x, *, tm=1024, tn=512):
    n, d = x.shape
    tm = min(tm, n)
    tn = min(tn, d)
    wa, wb = ao.shape[1], bo.shape[1]
    return pl.pallas_call(
        _out_proj_kernel,
        grid=(n // tm, d // tn),
        in_specs=[
            pl.BlockSpec((tm, wa), lambda i, j: (i, 0)),
            pl.BlockSpec((tm, wb), lambda i, j: (i, 0)),
            pl.BlockSpec((wa, tn), lambda i, j: (0, j)),
            pl.BlockSpec((wb, tn), lambda i, j: (wa // wb, j)),
            pl.BlockSpec((tm, tn), lambda i, j: (i, j)),
        ],
        out_specs=pl.BlockSpec((tm, tn), lambda i, j: (i, j)),
        out_shape=jax.ShapeDtypeStruct((n, d), F32),
        compiler_params=_params("parallel", "parallel"),
        name="out_proj",
    )(ao, bo, w_out, w_out, x)


def _swiglu_act(g, u):
    return (g * jax.nn.sigmoid(g)) * u


def _ffn_kernel(x_ref, gn_ref, wg_ref, wu_ref, wd_ref, o_ref, h_ref):
    @pl.when(pl.program_id(1) == 0)
    def _():
        xf = x_ref[...]
        h_ref[...] = _rms_rows(xf, gn_ref[...]).astype(h_ref.dtype)
        o_ref[...] = xf

    h = h_ref[...]
    g = jnp.dot(h, wg_ref[...], preferred_element_type=F32)
    u = jnp.dot(h, wu_ref[...], preferred_element_type=F32)
    a = _swiglu_act(g, u).astype(BF16)
    o_ref[...] += jnp.dot(a, wd_ref[...], preferred_element_type=F32)


def _ffn_dense(x, gn, wg, wu, wd, *, tm=512, tf=512):
    n, d = x.shape
    f = wg.shape[1]
    tm = min(tm, n)
    return pl.pallas_call(
        _ffn_kernel,
        grid=(n // tm, f // tf),
        in_specs=[
            pl.BlockSpec((tm, d), lambda i, j: (i, 0)),
            pl.BlockSpec((1, d), lambda i, j: (0, 0)),
            pl.BlockSpec((d, tf), lambda i, j: (0, j)),
            pl.BlockSpec((d, tf), lambda i, j: (0, j)),
            pl.BlockSpec((tf, d), lambda i, j: (j, 0)),
        ],
        out_specs=pl.BlockSpec((tm, d), lambda i, j: (i, 0)),
        out_shape=jax.ShapeDtypeStruct((n, d), F32),
        scratch_shapes=[pltpu.VMEM((tm, d), BF16)],
        compiler_params=_params("parallel", "arbitrary"),
        name="ffn_dense",
    )(x, gn.reshape(1, d), wg, wu, wd)


def _pad_ff(w, axis, mult):
    f = w.shape[axis]
    fp = -(-f // mult) * mult
    if fp == f:
        return w
    pad = [(0, 0)] * w.ndim
    pad[axis] = (0, fp - f)
    return jnp.pad(w, pad)


def _router_kernel(x_ref, gn_ref, r_ref, h_ref, idx_ref, gate_ref):
    h = _rms_rows(x_ref[...], gn_ref[...])
    h_ref[...] = h
    logits = jnp.dot(h.astype(BF16), r_ref[...], preferred_element_type=F32)
    lane = lax.broadcasted_iota(jnp.int32, logits.shape, 1)
    m1 = jnp.max(logits, axis=-1, keepdims=True)
    i1 = jnp.min(jnp.where(logits == m1, lane, N_EXPERTS), axis=-1, keepdims=True)
    rest = jnp.where(lane == i1, -jnp.inf, logits)
    m2 = jnp.max(rest, axis=-1, keepdims=True)
    i2 = jnp.min(jnp.where(rest == m2, lane, N_EXPERTS), axis=-1, keepdims=True)
    e2 = jnp.exp(m2 - m1)
    den = 1.0 + e2
    first = lax.broadcasted_iota(jnp.int32, idx_ref.shape, 1) == 0
    idx_ref[...] = jnp.where(first, i1, i2)
    gate_ref[...] = jnp.where(first, 1.0 / den, e2 / den)


def _router(x, gn, router, *, tm=512):
    n, d = x.shape
    tm = min(tm, n)
    return pl.pallas_call(
        _router_kernel,
        grid=(n // tm,),
        in_specs=[
            pl.BlockSpec((tm, d), lambda i: (i, 0)),
            pl.BlockSpec((1, d), lambda i: (0, 0)),
            pl.BlockSpec((d, N_EXPERTS), lambda i: (0, 0)),
        ],
        out_specs=[
            pl.BlockSpec((tm, d), lambda i: (i, 0)),
            pl.BlockSpec((tm, TOP_K), lambda i: (i, 0)),
            pl.BlockSpec((tm, TOP_K), lambda i: (i, 0)),
        ],
        out_shape=[
            jax.ShapeDtypeStruct((n, d), F32),
            jax.ShapeDtypeStruct((n, TOP_K), jnp.int32),
            jax.ShapeDtypeStruct((n, TOP_K), F32),
        ],
        compiler_params=_params("parallel"),
        name="moe_router",
    )(x, gn.reshape(1, d), router)


DMA_PRIORITIES = 2


def _gather_rows(idx_ref, src_ref, dst_ref, sem, rows):
    def start(p, carry):
        for k in range(DMA_PRIORITIES):
            r = DMA_PRIORITIES * p + k
            pltpu.make_async_copy(src_ref.at[pl.ds(idx_ref[r], 1)], dst_ref.at[pl.ds(r, 1)], sem).start(priority=k)
        return carry

    lax.fori_loop(0, rows // DMA_PRIORITIES, start, 0, unroll=4)
    pltpu.make_async_copy(src_ref.at[pl.ds(0, rows)], dst_ref.at[pl.ds(0, rows)], sem).wait()


def _gather_kernel(idx_ref, src_ref, o_ref, buf_ref, sem, *, rows):
    _gather_rows(idx_ref, src_ref, buf_ref, sem, rows)
    o_ref[...] = buf_ref[...].astype(o_ref.dtype)


def _gather(src, idx, *, rows, out_dtype):
    r = idx.shape[0]
    d = src.shape[1]
    return pl.pallas_call(
        functools.partial(_gather_kernel, rows=rows),
        grid=(r // rows,),
        in_specs=[
            pl.BlockSpec((rows,), lambda i: (i,), memory_space=pltpu.SMEM),
            pl.BlockSpec(memory_space=pl.ANY),
        ],
        out_specs=pl.BlockSpec((rows, d), lambda i: (i, 0)),
        out_shape=jax.ShapeDtypeStruct((r, d), out_dtype),
        scratch_shapes=[pltpu.VMEM((rows, d), src.dtype), pltpu.SemaphoreType.DMA],
        compiler_params=_params("arbitrary"),
        name="moe_gather",
    )(idx, src)


def _moe_ffn_kernel(te_ref, nu_ref, h_ref, wg_ref, wu_ref, wd_ref, o_ref):
    i = pl.program_id(0)
    j = pl.program_id(1)

    @pl.when(j == 0)
    def _():
        o_ref[...] = jnp.zeros(o_ref.shape, F32)

    @pl.when(i < nu_ref[0])
    def _():
        h = h_ref[...]
        g = jnp.dot(h, wg_ref[...], preferred_element_type=F32)
        u = jnp.dot(h, wu_ref[...], preferred_element_type=F32)
        a = _swiglu_act(g, u).astype(BF16)
        o_ref[...] += jnp.dot(a, wd_ref[...], preferred_element_type=F32)


def _moe_ffn(tile_expert, n_used, h_sorted, wg, wu, wd, *, tm, tf=1024):
    r, d = h_sorted.shape
    f = wg.shape[2]
    nf = f // tf
    jj = lambda i, j, te, nu: jnp.where(i < nu[0], j, nf - 1)
    return pl.pallas_call(
        _moe_ffn_kernel,
        grid_spec=pltpu.PrefetchScalarGridSpec(
            num_scalar_prefetch=2,
            grid=(r // tm, nf),
            in_specs=[
                pl.BlockSpec((tm, d), lambda i, j, te, nu: (i, 0)),
                pl.BlockSpec((None, d, tf), lambda i, j, te, nu: (te[i], 0, jj(i, j, te, nu))),
                pl.BlockSpec((None, d, tf), lambda i, j, te, nu: (te[i], 0, jj(i, j, te, nu))),
                pl.BlockSpec((None, tf, d), lambda i, j, te, nu: (te[i], jj(i, j, te, nu), 0)),
            ],
            out_specs=pl.BlockSpec((tm, d), lambda i, j, te, nu: (i, 0)),
        ),
        out_shape=jax.ShapeDtypeStruct((r, d), F32),
        compiler_params=_params("arbitrary", "arbitrary"),
        name="moe_experts",
    )(tile_expert, n_used, h_sorted, wg, wu, wd)


def _combine_kernel(idx_ref, y_ref, x_ref, gate_ref, o_ref, buf_ref, sem, *, tc):
    _gather_rows(idx_ref, y_ref, buf_ref, sem, TOP_K * tc)
    gate = gate_ref[...]
    o_ref[...] = (x_ref[...] + gate[:, 0:1] * buf_ref[pl.ds(0, tc), :]
                  + gate[:, 1:2] * buf_ref[pl.ds(tc, tc), :])


def _combine(y_sorted, pos, gate, x, *, tc=256):
    n, d = x.shape
    tc = min(tc, n)
    idx = pos.reshape(n // tc, tc, TOP_K).transpose(0, 2, 1).reshape(-1)
    return pl.pallas_call(
        functools.partial(_combine_kernel, tc=tc),
        grid=(n // tc,),
        in_specs=[
            pl.BlockSpec((TOP_K * tc,), lambda i: (i,), memory_space=pltpu.SMEM),
            pl.BlockSpec(memory_space=pl.ANY),
            pl.BlockSpec((tc, d), lambda i: (i, 0)),
            pl.BlockSpec((tc, TOP_K), lambda i: (i, 0)),
        ],
        out_specs=pl.BlockSpec((tc, d), lambda i: (i, 0)),
        out_shape=jax.ShapeDtypeStruct((n, d), F32),
        scratch_shapes=[pltpu.VMEM((TOP_K * tc, d), F32), pltpu.SemaphoreType.DMA],
        compiler_params=_params("arbitrary"),
        name="moe_combine",
    )(idx, y_sorted, x, gate)


def _route_plan(idx, tm):
    n = idx.shape[0]
    a = n * TOP_K
    ntiles = a // tm + N_EXPERTS
    experts = jnp.arange(N_EXPERTS, dtype=jnp.int32)
    e_flat = idx.reshape(a)
    onehot = e_flat[:, None] == experts[None, :]
    order = jnp.argsort(e_flat, stable=True).astype(jnp.int32)
    counts = jnp.sum(onehot, axis=0).astype(jnp.int32)
    padded = ((counts + tm - 1) // tm) * tm
    pad_end = jnp.cumsum(padded)
    pad_start = pad_end - padded
    raw_start = jnp.cumsum(counts) - counts
    tile_start = jnp.arange(ntiles, dtype=jnp.int32) * tm
    tile_expert = jnp.minimum(jnp.sum(tile_start[:, None] >= pad_end[None, :], axis=1), N_EXPERTS - 1).astype(jnp.int32)
    rank = (tile_start - pad_start[tile_expert])[:, None] + jnp.arange(tm, dtype=jnp.int32)[None, :]
    valid = rank < counts[tile_expert][:, None]
    src = jnp.clip(raw_start[tile_expert][:, None] + rank, 0, a - 1)
    row_token = jnp.where(valid, order[src] // TOP_K, 0).reshape(ntiles * tm)
    sorted_pos = jnp.argsort(order).astype(jnp.int32)
    shift = jnp.sum(jnp.where(onehot, (pad_start - raw_start)[None, :], 0), axis=1)
    pos = sorted_pos + shift
    n_used = (pad_end[-1] // tm).reshape(1).astype(jnp.int32)
    return row_token, pos.reshape(n, TOP_K), tile_expert, n_used


def _moe(x, gn, router, wg, wu, wd, *, tm=512):
    n = x.shape[0]
    tm = min(tm, n)
    h, idx, gate = _router(x, gn, router)
    row_token, pos, tile_expert, n_used = _route_plan(idx, tm)
    h_sorted = _gather(h, row_token, rows=tm, out_dtype=BF16)
    y_sorted = _moe_ffn(tile_expert, n_used, h_sorted, wg, wu, wd, tm=tm)
    return _combine(y_sorted, pos, gate, x)


def kernel(x, attn_norm, w_in, diff_q_norm, diff_k_norm, diff_lambda, diff_subln, gqa_q_norm, gqa_k_norm, w_out, ffn_norm, dense_w_gate, dense_w_up, dense_w_down, moe_router, moe_w_gate, moe_w_up, moe_w_down):
    batch, seq, d = x.shape
    depth = w_in.shape[0]
    n = batch * seq
    xs = x.reshape(n, d).astype(F32)
    tables = _position_tables(seq, min(ATTN_TILE, seq))
    slopes = tuple(2.0 ** (-8.0 * (h + 1) / DIFF_HEADS) for h in range(DIFF_HEADS))
    ff_tile = 512
    bound_margin = 1.02

    for l in range(depth):
        amax = lambda g: jnp.max(jnp.abs(g.astype(F32)))
        diff_bound = amax(diff_q_norm[l]) * amax(diff_k_norm[l]) * (DIFF_HALF ** 0.5 * bound_margin)
        gqa_bound = amax(gqa_q_norm[l]) * amax(gqa_k_norm[l]) * (HEAD_DIM ** 0.5 * bound_margin)
        diff_ok = diff_bound <= SHIFT_MAX
        diff_shift = jnp.where(diff_ok, diff_bound, 0.0)
        aq1, aq2, ak1, ak2, av, bq, bk, bv = _in_proj(
            xs, attn_norm[l], w_in[l].astype(BF16), tables, diff_shift,
            diff_q_norm[l], diff_k_norm[l], gqa_q_norm[l], gqa_k_norm[l], seq=seq, slopes=slopes)
        lam_init = 0.8 - 0.6 * math.exp(-0.3 * l)
        lp = diff_lambda[l].astype(F32)
        lam = jnp.exp(jnp.sum(lp[0] * lp[1])) - jnp.exp(jnp.sum(lp[2] * lp[3])) + lam_init
        ao = _diff_attention(aq1, aq2, ak1, ak2, av, lam, diff_subln[l], diff_ok,
                             batch=batch, seq=seq, out_scale=1.0 - lam_init)
        bo = _gqa_attention(bq, bk, bv, gqa_bound, batch=batch, seq=seq)
        xs = _out_proj(ao, bo, w_out[l].astype(BF16), xs)

        i = l // 2
        if l % 2 == 0:
            wg = _pad_ff(dense_w_gate[i].astype(BF16), 1, ff_tile)
            wu = _pad_ff(dense_w_up[i].astype(BF16), 1, ff_tile)
            wd = _pad_ff(dense_w_down[i].astype(BF16), 0, ff_tile)
            xs = _ffn_dense(xs, ffn_norm[l], wg, wu, wd, tf=ff_tile)
        else:
            xs = _moe(xs, ffn_norm[l], moe_router[i].astype(BF16), moe_w_gate[i].astype(BF16),
                      moe_w_up[i].astype(BF16), moe_w_down[i].astype(BF16))
    return xs.reshape(batch, seq, d)
```

```python
import functools
import math

import numpy as np
import jax
import jax.numpy as jnp
from jax import lax
from jax.experimental import pallas as pl
from jax.experimental.pallas import tpu as pltpu

HEAD_DIM = 128
DIFF_HALF = HEAD_DIM // 2
DIFF_HEADS = 8
GQA_HEADS = 8
GQA_KV_HEADS = 2
GQA_GROUP = GQA_HEADS // GQA_KV_HEADS
DIFF_WIDTH = DIFF_HEADS * HEAD_DIM
GQA_WIDTH = GQA_HEADS * HEAD_DIM
GQA_KV_WIDTH = GQA_KV_HEADS * HEAD_DIM
GRID_W = 64
ROPE_THETA = 10000.0
ROPE_AXIS_DIM = HEAD_DIM // 2
N_EXPERTS = 8
TOP_K = 2
EPS = 1e-6

LANES = 128
VMEM_LIMIT_BYTES = 52 * 1024 * 1024

BF16 = jnp.bfloat16
F32 = jnp.float32


def _params(*sem):
    return pltpu.CompilerParams(dimension_semantics=sem, vmem_limit_bytes=VMEM_LIMIT_BYTES)


def _rms_rows(xf, g):
    ms = jnp.mean(xf * xf, axis=-1, keepdims=True)
    return (xf * lax.rsqrt(ms + EPS)) * g


ATTN_TILE = 512
DIFF_QSUB = 4
GQA_QTILE = 1024
GQA_KTILE = 1024
AUG0 = DIFF_HALF
LOG2E = 1.4426950408889634


IN_PROJ_TN = 512


def _in_proj_kernel(shift_ref, x_ref, g_ref, w_ref, cos_ref, sin_ref, qaug_ref, kaug_ref,
                    dq_ref, dk_ref, gq_ref, gk_ref,
                    aq1_ref, aq2_ref, ak1_ref, ak2_ref, av_ref, bq_ref, bk_ref, bv_ref, n_ref, *, slopes):
    j = pl.program_id(1)

    @pl.when(j == 0)
    def _():
        n_ref[...] = _rms_rows(x_ref[...], g_ref[...]).astype(n_ref.dtype)

    acc = jnp.dot(n_ref[...], w_ref[...], preferred_element_type=F32)
    heads_per_tile = IN_PROJ_TN // HEAD_DIM
    chunk = lambda c: acc[:, c * HEAD_DIM:(c + 1) * HEAD_DIM]
    lanes = lambda c: slice(c * HEAD_DIM, (c + 1) * HEAD_DIM)

    lane = lax.broadcasted_iota(jnp.int32, (1, HEAD_DIM), 1)
    lo = lane < DIFF_HALF
    even = (lane & 1) == 0

    def half_rms(x, g):
        sq = x * x
        s_lo = jnp.sum(jnp.where(lo, sq, 0.0), axis=-1, keepdims=True)
        s_hi = jnp.sum(jnp.where(lo, 0.0, sq), axis=-1, keepdims=True)
        ms = jnp.where(lo, s_lo, s_hi) * (1.0 / DIFF_HALF)
        return (x * lax.rsqrt(ms + EPS)) * g

    def rope(y):
        sw = jnp.where(even, pltpu.roll(y, HEAD_DIM - 1, 1), pltpu.roll(y, 1, 1))
        return y * cos_ref[...] + sw * sin_ref[...]

    for jq in range(DIFF_HEADS // heads_per_tile):
        @pl.when(j == jq)
        def _():
            tail = jnp.where(lane == AUG0 + 4, 1.0, 0.0) + jnp.where(lane == AUG0 + 5, -shift_ref[0], 0.0)
            for c in range(heads_per_tile):
                q = half_rms(chunk(c), dq_ref[...]) * (DIFF_HALF ** -0.5)
                qa = qaug_ref[...] * slopes[jq * heads_per_tile + c] + tail
                aq1_ref[:, lanes(c)] = jnp.where(lo, q, qa).astype(BF16)
                aq2_ref[:, lanes(c)] = jnp.where(lo, pltpu.roll(q, DIFF_HALF, 1), qa).astype(BF16)

    @pl.when((j == 2) | (j == 3))
    def _():
        kaug = kaug_ref[...]
        for c in range(heads_per_tile):
            k = half_rms(chunk(c), dk_ref[...])
            ak1_ref[:, lanes(c)] = jnp.where(lo, k, kaug).astype(BF16)
            ak2_ref[:, lanes(c)] = jnp.where(lo, pltpu.roll(k, DIFF_HALF, 1), kaug).astype(BF16)

    @pl.when((j == 4) | (j == 5))
    def _():
        av_ref[...] = acc.astype(BF16)

    @pl.when((j == 6) | (j == 7))
    def _():
        for c in range(heads_per_tile):
            q = rope(_rms_rows(chunk(c), gq_ref[...])) * (HEAD_DIM ** -0.5 * LOG2E)
            bq_ref[:, lanes(c)] = q.astype(BF16)

    @pl.when(j == 8)
    def _():
        for c in range(GQA_KV_HEADS):
            bk_ref[:, lanes(c)] = rope(_rms_rows(chunk(c), gk_ref[...])).astype(BF16)
            bv_ref[:, lanes(c)] = chunk(GQA_KV_HEADS + c).astype(BF16)


def _in_proj(x, g, w, tables, shift, dq, dk, gq, gk, *, seq, slopes, tm=1024):
    n, d = x.shape
    tn = IN_PROJ_TN
    assert w.shape[1] == 3 * DIFF_WIDTH + GQA_WIDTH + 2 * GQA_KV_WIDTH and 2 * GQA_KV_WIDTH == tn
    tm = min(tm, seq)
    nt = seq // tm
    pos = lambda i, j: (i % nt, 0)
    fixed = lambda i, j: (0, 0)
    wide = lambda j0: (lambda i, j: (i, jnp.clip(j - j0, 0, 1)))
    dq2 = jnp.concatenate([dq, dq]).reshape(1, HEAD_DIM)
    dk2 = jnp.concatenate([dk, dk]).reshape(1, HEAD_DIM)
    kv = (n, GQA_KV_WIDTH)
    shapes = [(n, DIFF_WIDTH)] * 5 + [(n, GQA_WIDTH), kv, kv]
    starts = [0, 0, 2, 2, 4, 6]
    return pl.pallas_call(
        functools.partial(_in_proj_kernel, slopes=slopes),
        grid=(n // tm, w.shape[1] // tn),
        in_specs=[
            pl.BlockSpec(memory_space=pltpu.SMEM),
            pl.BlockSpec((tm, d), lambda i, j: (i, 0)),
            pl.BlockSpec((1, d), fixed),
            pl.BlockSpec((d, tn), lambda i, j: (0, j)),
        ] + [pl.BlockSpec((tm, HEAD_DIM), pos)] * 4 + [pl.BlockSpec((1, HEAD_DIM), fixed)] * 4,
        out_specs=[pl.BlockSpec((tm, tn), wide(j0)) for j0 in starts]
        + [pl.BlockSpec((tm, GQA_KV_WIDTH), lambda i, j: (i, 0))] * 2,
        out_shape=[jax.ShapeDtypeStruct(sh, BF16) for sh in shapes],
        scratch_shapes=[pltpu.VMEM((tm, d), BF16)],
        compiler_params=_params("parallel", "arbitrary"),
        name="in_proj_heads",
    )(shift.reshape(1), x, g.reshape(1, d), w, *tables, dq2, dk2, gq.reshape(1, HEAD_DIM), gk.reshape(1, HEAD_DIM))


def _position_tables(seq, tile):
    rows_n = seq // GRID_W
    row = jnp.repeat(jnp.arange(rows_n, dtype=F32), GRID_W, total_repeat_length=seq)
    col = jnp.tile(jnp.arange(GRID_W, dtype=F32), rows_n)
    inv = ROPE_THETA ** (-jnp.arange(0, ROPE_AXIS_DIM, 2, dtype=F32) / ROPE_AXIS_DIM)
    ang = jnp.concatenate([row[:, None] * inv, col[:, None] * inv], axis=-1)
    cos = jnp.repeat(jnp.cos(ang), 2, axis=-1)
    sin = jnp.sin(ang)
    sin_signed = jnp.stack([-sin, sin], axis=-1).reshape(seq, HEAD_DIM)
    loc = np.arange(seq) % tile
    hi = (loc // 16).astype(np.float32)
    lo = (loc % 16).astype(np.float32)
    qaug = np.zeros((seq, HEAD_DIM), np.float32)
    kaug = np.zeros((seq, HEAD_DIM), np.float32)
    qaug[:, AUG0 + 0] = 16.0
    qaug[:, AUG0 + 1] = 1.0
    qaug[:, AUG0 + 2] = -16.0 * hi
    qaug[:, AUG0 + 3] = -lo
    kaug[:, AUG0 + 0] = hi
    kaug[:, AUG0 + 1] = lo
    kaug[:, AUG0 + 2:AUG0 + 6] = 1.0
    return cos, sin_signed, jnp.asarray(qaug), jnp.asarray(kaug)


SHIFT_MAX = 40.0
UNDERFLOW = 106.0


def _qk(q, k):
    return lax.dot_general(q, k, (((1,), (1,)), ((), ())), preferred_element_type=F32)


def _lane_fold(p):
    out = p[:, :LANES]
    for c in range(1, p.shape[1] // LANES):
        out = out + p[:, c * LANES:(c + 1) * LANES]
    return out


def _online_update(s, v, m_ref, l_ref, acc_ref, idx, exp_fn):
    m_old = m_ref[idx]
    m_new = jnp.maximum(m_old, jnp.max(s, axis=-1, keepdims=True))
    alpha = exp_fn(m_old - m_new)
    p = exp_fn(s - m_new)
    l_ref[idx] = alpha * l_ref[idx] + jnp.sum(p, axis=-1, keepdims=True)
    acc_ref[idx] = alpha * acc_ref[idx] + jnp.dot(p.astype(BF16), v, preferred_element_type=F32)
    m_ref[idx] = m_new


def _shifted_update(p, v, l_ref, acc_ref, idx):
    l_ref[idx] += _lane_fold(p)
    acc_ref[idx] += jnp.dot(p.astype(BF16), v, preferred_element_type=F32)


def _row_total(l_ref, idx, stable):
    return l_ref[idx] if stable else jnp.sum(l_ref[idx], axis=-1, keepdims=True)


def _diff_attn_kernel(lam_ref, q1_ref, q2_ref, k1_ref, k2_ref, v_ref, g_ref, o_ref, *scratch,
                      slope, t, qsub, nk, window, out_scale, stable):
    if stable:
        m_ref, l_ref, acc_ref = scratch
        m_ref[...] = jnp.full(m_ref.shape, -jnp.inf, F32)
    else:
        l_ref, acc_ref = scratch
    l_ref[...] = jnp.zeros(l_ref.shape, F32)
    acc_ref[...] = jnp.zeros(acc_ref.shape, F32)
    q0 = qsub * pl.program_id(1)

    def tile(sub, d, ki, bias):
        rows = pl.ds(sub * t, t)
        keys = pl.ds(pl.multiple_of(ki * t, t), t)
        lane = lax.broadcasted_iota(jnp.int32, (1, HEAD_DIM), 1)
        sig = jnp.where(d > 0, 1.0, jnp.where(d < 0, -1.0, 0.0))
        tval = (-slope * t) * jnp.abs(d).astype(F32)
        mult = jnp.where(lane < AUG0, 1.0,
                         jnp.where(lane < AUG0 + 4, sig, jnp.where(lane == AUG0 + 4, tval, 1.0))).astype(BF16)
        v = v_ref[keys, :]
        for c, (q_ref, k_ref) in enumerate(((q1_ref, k1_ref), (q2_ref, k2_ref))):
            s = _qk(q_ref[rows, :] * mult, k_ref[keys, :])
            if bias is not None:
                s = s + bias
            if stable:
                _online_update(s, v, m_ref, l_ref, acc_ref, qsub * c + sub, jnp.exp)
            else:
                _shifted_update(jnp.exp(s), v, l_ref, acc_ref, qsub * c + sub)

    def key_tile(ki, carry):
        offs = [q0 + sub - ki for sub in range(qsub)]
        offdiag = [(d != 0) if window is None else (d != 0) & (jnp.abs(d) <= window) for d in offs]
        all_off = functools.reduce(jnp.logical_and, offdiag)

        @pl.when(all_off)
        def _():
            for sub, d in enumerate(offs):
                tile(sub, d, ki, None)

        for sub, d in enumerate(offs):
            @pl.when(offdiag[sub] & jnp.logical_not(all_off))
            def _():
                tile(sub, d, ki, None)

            @pl.when(d == 0)
            def _():
                r = lax.broadcasted_iota(jnp.int32, (t, t), 0)
                c = lax.broadcasted_iota(jnp.int32, (t, t), 1)
                tile(sub, d, ki, jnp.abs(r - c).astype(F32) * (-slope))
        return carry

    if window is None:
        lo, hi = 0, nk
    else:
        lo = jnp.maximum(q0 - window, 0)
        hi = jnp.minimum(q0 + qsub + window, nk)
    lax.fori_loop(lo, hi, key_tile, 0)

    for sub in range(qsub):
        o1 = acc_ref[sub] / _row_total(l_ref, sub, stable)
        o2 = acc_ref[qsub + sub] / _row_total(l_ref, qsub + sub, stable)
        o = _rms_rows(o1 - lam_ref[0] * o2, g_ref[...]) * out_scale
        o_ref[pl.ds(sub * t, t), :] = o.astype(o_ref.dtype)


def _diff_window(slope, tile, nk):
    w = int(math.floor((UNDERFLOW / slope - 1.0) / tile)) + 1
    return None if 2 * w + 1 >= nk else w


def _diff_attention_head(h, aq1, aq2, ak1, ak2, av, lam, subln, *, batch, seq, out_scale, stable):
    n = aq1.shape[0]
    t = min(ATTN_TILE, seq)
    nk = seq // t
    qsub = DIFF_QSUB if nk % DIFF_QSUB == 0 else 1
    nqb = nk // qsub
    slope = 2.0 ** (-8.0 * (h + 1) / DIFF_HEADS)
    window = None if stable else _diff_window(slope, t, nk)
    qmap = lambda b, qb: (b * nqb + qb, h)
    kmap = lambda b, qb: (b, h)
    kern = functools.partial(_diff_attn_kernel, slope=slope, t=t, qsub=qsub, nk=nk, window=window,
                             out_scale=out_scale, stable=stable)
    stat = [pltpu.VMEM((2 * qsub, t, 1), F32)] * 2 if stable else [pltpu.VMEM((2 * qsub, t, LANES), F32)]
    return pl.pallas_call(
        kern,
        grid=(batch, nqb),
        in_specs=[
            pl.BlockSpec(memory_space=pltpu.SMEM),
            pl.BlockSpec((qsub * t, HEAD_DIM), qmap),
            pl.BlockSpec((qsub * t, HEAD_DIM), qmap),
            pl.BlockSpec((seq, HEAD_DIM), kmap),
            pl.BlockSpec((seq, HEAD_DIM), kmap),
            pl.BlockSpec((seq, HEAD_DIM), kmap),
            pl.BlockSpec((1, HEAD_DIM), lambda b, qb: (0, 0)),
        ],
        out_specs=pl.BlockSpec((qsub * t, HEAD_DIM), lambda b, qb: (b * nqb + qb, 0)),
        out_shape=jax.ShapeDtypeStruct((n, HEAD_DIM), BF16),
        scratch_shapes=stat + [pltpu.VMEM((2 * qsub, t, HEAD_DIM), F32)],
        compiler_params=_params("parallel", "parallel"),
        name=f"diff_attention_h{h}" + ("_stable" if stable else ""),
    )(lam.reshape(1), aq1, aq2, ak1, ak2, av, subln.reshape(1, HEAD_DIM))


def _diff_attention(aq1, aq2, ak1, ak2, av, lam, subln, bound_ok, *, batch, seq, out_scale):
    def run(stable):
        def f(aq1, aq2, ak1, ak2, av, lam, subln):
            heads = [_diff_attention_head(h, aq1, aq2, ak1, ak2, av, lam, subln, batch=batch, seq=seq,
                                          out_scale=out_scale, stable=stable) for h in range(DIFF_HEADS)]
            return jnp.concatenate(heads, axis=-1)
        return f
    return lax.cond(bound_ok, run(False), run(True), aq1, aq2, ak1, ak2, av, lam, subln)


def _gqa_attn_kernel(shift_ref, q_ref, k_ref, v_ref, o_ref, *scratch, tk, nk, stable):
    if stable:
        m_ref, l_ref, acc_ref = scratch
        m_ref[...] = jnp.full(m_ref.shape, -jnp.inf, F32)
    else:
        l_ref, acc_ref = scratch
    l_ref[...] = jnp.zeros(l_ref.shape, F32)
    acc_ref[...] = jnp.zeros(acc_ref.shape, F32)

    def key_tile(ki, carry):
        keys = pl.ds(pl.multiple_of(ki * tk, tk), tk)
        k = k_ref[keys, :]
        v = v_ref[keys, :]
        for r in range(GQA_GROUP):
            s = _qk(q_ref[:, r * HEAD_DIM:(r + 1) * HEAD_DIM], k)
            if stable:
                _online_update(s, v, m_ref, l_ref, acc_ref, r, jnp.exp2)
            else:
                _shifted_update(jnp.exp2(s - shift_ref[0]), v, l_ref, acc_ref, r)
        return carry

    lax.fori_loop(0, nk, key_tile, 0)

    for r in range(GQA_GROUP):
        o = acc_ref[r] / _row_total(l_ref, r, stable)
        o_ref[:, r * HEAD_DIM:(r + 1) * HEAD_DIM] = o.astype(o_ref.dtype)


def _gqa_attention_call(shift, bq, bk, bv, *, batch, seq, stable):
    n = bq.shape[0]
    tk = min(GQA_KTILE, seq)
    tq = min(GQA_QTILE, seq)
    nq, nk = seq // tq, seq // tk
    gw = GQA_GROUP * HEAD_DIM
    qmap = lambda b, g, qi: (b * nq + qi, g)
    kmap = lambda b, g, qi: (b, g)
    stat = [pltpu.VMEM((GQA_GROUP, tq, 1), F32)] * 2 if stable else [pltpu.VMEM((GQA_GROUP, tq, LANES), F32)]
    return pl.pallas_call(
        functools.partial(_gqa_attn_kernel, tk=tk, nk=nk, stable=stable),
        grid=(batch, GQA_KV_HEADS, nq),
        in_specs=[
            pl.BlockSpec(memory_space=pltpu.SMEM),
            pl.BlockSpec((tq, gw), qmap),
            pl.BlockSpec((seq, HEAD_DIM), kmap),
            pl.BlockSpec((seq, HEAD_DIM), kmap),
        ],
        out_specs=pl.BlockSpec((tq, gw), qmap),
        out_shape=jax.ShapeDtypeStruct((n, GQA_WIDTH), BF16),
        scratch_shapes=stat + [pltpu.VMEM((GQA_GROUP, tq, HEAD_DIM), F32)],
        compiler_params=_params("parallel", "parallel", "parallel"),
        name="gqa_attention" + ("_stable" if stable else ""),
    )(shift.reshape(1), bq, bk, bv)


def _gqa_attention(bq, bk, bv, bound, *, batch, seq):
    shift = bound * LOG2E
    return lax.cond(
        bound <= SHIFT_MAX,
        functools.partial(_gqa_attention_call, batch=batch, seq=seq, stable=False),
        functools.partial(_gqa_attention_call, batch=batch, seq=seq, stable=True),
        shift, bq, bk, bv)


def _out_proj_kernel(a_ref, b_ref, wa_ref, wb_ref, x_ref, o_ref):
    acc = jnp.dot(a_ref[...], wa_ref[...], preferred_element_type=F32)
    acc = acc + jnp.dot(b_ref[...], wb_ref[...], preferred_element_type=F32)
    o_ref[...] = x_ref[...] + acc


def _out_proj(ao, bo, w_out, x, *, tm=1024, tn=512):
    n, d = x.shape
    tm = min(tm, n)
    tn = min(tn, d)
    wa, wb = ao.shape[1], bo.shape[1]
    return pl.pallas_call(
        _out_proj_kernel,
        grid=(n // tm, d // tn),
        in_specs=[
            pl.BlockSpec((tm, wa), lambda i, j: (i, 0)),
            pl.BlockSpec((tm, wb), lambda i, j: (i, 0)),
            pl.BlockSpec((wa, tn), lambda i, j: (0, j)),
            pl.BlockSpec((wb, tn), lambda i, j: (wa // wb, j)),
            pl.BlockSpec((tm, tn), lambda i, j: (i, j)),
        ],
        out_specs=pl.BlockSpec((tm, tn), lambda i, j: (i, j)),
        out_shape=jax.ShapeDtypeStruct((n, d), F32),
        compiler_params=_params("parallel", "parallel"),
        name="out_proj",
    )(ao, bo, w_out, w_out, x)


def _swiglu_act(g, u):
    return (g * jax.nn.sigmoid(g)) * u


def _ffn_kernel(x_ref, gn_ref, wg_ref, wu_ref, wd_ref, o_ref, h_ref):
    @pl.when(pl.program_id(1) == 0)
    def _():
        xf = x_ref[...]
        h_ref[...] = _rms_rows(xf, gn_ref[...]).astype(h_ref.dtype)
        o_ref[...] = xf

    h = h_ref[...]
    g = jnp.dot(h, wg_ref[...], preferred_element_type=F32)
    u = jnp.dot(h, wu_ref[...], preferred_element_type=F32)
    a = _swiglu_act(g, u).astype(BF16)
    o_ref[...] += jnp.dot(a, wd_ref[...], preferred_element_type=F32)


def _ffn_dense(x, gn, wg, wu, wd, *, tm=512, tf=512):
    n, d = x.shape
    f = wg.shape[1]
    tm = min(tm, n)
    return pl.pallas_call(
        _ffn_kernel,
        grid=(n // tm, f // tf),
        in_specs=[
            pl.BlockSpec((tm, d), lambda i, j: (i, 0)),
            pl.BlockSpec((1, d), lambda i, j: (0, 0)),
            pl.BlockSpec((d, tf), lambda i, j: (0, j)),
            pl.BlockSpec((d, tf), lambda i, j: (0, j)),
            pl.BlockSpec((tf, d), lambda i, j: (j, 0)),
        ],
        out_specs=pl.BlockSpec((tm, d), lambda i, j: (i, 0)),
        out_shape=jax.ShapeDtypeStruct((n, d), F32),
        scratch_shapes=[pltpu.VMEM((tm, d), BF16)],
        compiler_params=_params("parallel", "arbitrary"),
        name="ffn_dense",
    )(x, gn.reshape(1, d), wg, wu, wd)


def _pad_ff(w, axis, mult):
    f = w.shape[axis]
    fp = -(-f // mult) * mult
    if fp == f:
        return w
    pad = [(0, 0)] * w.ndim
    pad[axis] = (0, fp - f)
    return jnp.pad(w, pad)


def _router_kernel(x_ref, gn_ref, r_ref, h_ref, idx_ref, gate_ref):
    h = _rms_rows(x_ref[...], gn_ref[...])
    h_ref[...] = h
    logits = jnp.dot(h.astype(BF16), r_ref[...], preferred_element_type=F32)
    lane = lax.broadcasted_iota(jnp.int32, logits.shape, 1)
    m1 = jnp.max(logits, axis=-1, keepdims=True)
    i1 = jnp.min(jnp.where(logits == m1, lane, N_EXPERTS), axis=-1, keepdims=True)
    rest = jnp.where(lane == i1, -jnp.inf, logits)
    m2 = jnp.max(rest, axis=-1, keepdims=True)
    i2 = jnp.min(jnp.where(rest == m2, lane, N_EXPERTS), axis=-1, keepdims=True)
    e2 = jnp.exp(m2 - m1)
    den = 1.0 + e2
    first = lax.broadcasted_iota(jnp.int32, idx_ref.shape, 1) == 0
    idx_ref[...] = jnp.where(first, i1, i2)
    gate_ref[...] = jnp.where(first, 1.0 / den, e2 / den)


def _router(x, gn, router, *, tm=512):
    n, d = x.shape
    tm = min(tm, n)
    return pl.pallas_call(
        _router_kernel,
        grid=(n // tm,),
        in_specs=[
            pl.BlockSpec((tm, d), lambda i: (i, 0)),
            pl.BlockSpec((1, d), lambda i: (0, 0)),
            pl.BlockSpec((d, N_EXPERTS), lambda i: (0, 0)),
        ],
        out_specs=[
            pl.BlockSpec((tm, d), lambda i: (i, 0)),
            pl.BlockSpec((tm, TOP_K), lambda i: (i, 0)),
            pl.BlockSpec((tm, TOP_K), lambda i: (i, 0)),
        ],
        out_shape=[
            jax.ShapeDtypeStruct((n, d), F32),
            jax.ShapeDtypeStruct((n, TOP_K), jnp.int32),
            jax.ShapeDtypeStruct((n, TOP_K), F32),
        ],
        compiler_params=_params("parallel"),
        name="moe_router",
    )(x, gn.reshape(1, d), router)


def _gather_rows(idx_ref, src_ref, dst_ref, sem, rows):
    def start(r, carry):
        pltpu.make_async_copy(src_ref.at[pl.ds(idx_ref[r], 1)], dst_ref.at[pl.ds(r, 1)], sem).start()
        return carry

    lax.fori_loop(0, rows, start, 0, unroll=8)
    pltpu.make_async_copy(src_ref.at[pl.ds(0, rows)], dst_ref.at[pl.ds(0, rows)], sem).wait()


def _gather_kernel(idx_ref, src_ref, o_ref, buf_ref, sem, *, rows):
    _gather_rows(idx_ref, src_ref, buf_ref, sem, rows)
    o_ref[...] = buf_ref[...].astype(o_ref.dtype)


def _gather(src, idx, *, rows, out_dtype):
    r = idx.shape[0]
    d = src.shape[1]
    return pl.pallas_call(
        functools.partial(_gather_kernel, rows=rows),
        grid=(r // rows,),
        in_specs=[
            pl.BlockSpec((rows,), lambda i: (i,), memory_space=pltpu.SMEM),
            pl.BlockSpec(memory_space=pl.ANY),
        ],
        out_specs=pl.BlockSpec((rows, d), lambda i: (i, 0)),
        out_shape=jax.ShapeDtypeStruct((r, d), out_dtype),
        scratch_shapes=[pltpu.VMEM((rows, d), src.dtype), pltpu.SemaphoreType.DMA],
        compiler_params=_params("arbitrary"),
        name="moe_gather",
    )(idx, src)


def _moe_ffn_kernel(te_ref, nu_ref, h_ref, wg_ref, wu_ref, wd_ref, o_ref):
    i = pl.program_id(0)
    j = pl.program_id(1)

    @pl.when(j == 0)
    def _():
        o_ref[...] = jnp.zeros(o_ref.shape, F32)

    @pl.when(i < nu_ref[0])
    def _():
        h = h_ref[...]
        g = jnp.dot(h, wg_ref[...], preferred_element_type=F32)
        u = jnp.dot(h, wu_ref[...], preferred_element_type=F32)
        a = _swiglu_act(g, u).astype(BF16)
        o_ref[...] += jnp.dot(a, wd_ref[...], preferred_element_type=F32)


def _moe_ffn(tile_expert, n_used, h_sorted, wg, wu, wd, *, tm, tf=1024):
    r, d = h_sorted.shape
    f = wg.shape[2]
    nf = f // tf
    jj = lambda i, j, te, nu: jnp.where(i < nu[0], j, nf - 1)
    return pl.pallas_call(
        _moe_ffn_kernel,
        grid_spec=pltpu.PrefetchScalarGridSpec(
            num_scalar_prefetch=2,
            grid=(r // tm, nf),
            in_specs=[
                pl.BlockSpec((tm, d), lambda i, j, te, nu: (i, 0)),
                pl.BlockSpec((None, d, tf), lambda i, j, te, nu: (te[i], 0, jj(i, j, te, nu))),
                pl.BlockSpec((None, d, tf), lambda i, j, te, nu: (te[i], 0, jj(i, j, te, nu))),
                pl.BlockSpec((None, tf, d), lambda i, j, te, nu: (te[i], jj(i, j, te, nu), 0)),
            ],
            out_specs=pl.BlockSpec((tm, d), lambda i, j, te, nu: (i, 0)),
        ),
        out_shape=jax.ShapeDtypeStruct((r, d), F32),
        compiler_params=_params("arbitrary", "arbitrary"),
        name="moe_experts",
    )(tile_expert, n_used, h_sorted, wg, wu, wd)


def _combine_kernel(idx_ref, y_ref, x_ref, gate_ref, o_ref, buf_ref, sem, *, tc):
    _gather_rows(idx_ref, y_ref, buf_ref, sem, TOP_K * tc)
    gate = gate_ref[...]
    o_ref[...] = (x_ref[...] + gate[:, 0:1] * buf_ref[pl.ds(0, tc), :]
                  + gate[:, 1:2] * buf_ref[pl.ds(tc, tc), :])


def _combine(y_sorted, pos, gate, x, *, tc=512):
    n, d = x.shape
    tc = min(tc, n)
    idx = pos.reshape(n // tc, tc, TOP_K).transpose(0, 2, 1).reshape(-1)
    return pl.pallas_call(
        functools.partial(_combine_kernel, tc=tc),
        grid=(n // tc,),
        in_specs=[
            pl.BlockSpec((TOP_K * tc,), lambda i: (i,), memory_space=pltpu.SMEM),
            pl.BlockSpec(memory_space=pl.ANY),
            pl.BlockSpec((tc, d), lambda i: (i, 0)),
            pl.BlockSpec((tc, TOP_K), lambda i: (i, 0)),
        ],
        out_specs=pl.BlockSpec((tc, d), lambda i: (i, 0)),
        out_shape=jax.ShapeDtypeStruct((n, d), F32),
        scratch_shapes=[pltpu.VMEM((TOP_K * tc, d), F32), pltpu.SemaphoreType.DMA],
        compiler_params=_params("arbitrary"),
        name="moe_combine",
    )(idx, y_sorted, x, gate)


def _route_plan(idx, tm):
    n = idx.shape[0]
    a = n * TOP_K
    ntiles = a // tm + N_EXPERTS
    experts = jnp.arange(N_EXPERTS, dtype=jnp.int32)
    e_flat = idx.reshape(a)
    onehot = e_flat[:, None] == experts[None, :]
    order = jnp.argsort(e_flat, stable=True).astype(jnp.int32)
    counts = jnp.sum(onehot, axis=0).astype(jnp.int32)
    padded = ((counts + tm - 1) // tm) * tm
    pad_end = jnp.cumsum(padded)
    pad_start = pad_end - padded
    raw_start = jnp.cumsum(counts) - counts
    tile_start = jnp.arange(ntiles, dtype=jnp.int32) * tm
    tile_expert = jnp.minimum(jnp.sum(tile_start[:, None] >= pad_end[None, :], axis=1), N_EXPERTS - 1).astype(jnp.int32)
    rank = (tile_start - pad_start[tile_expert])[:, None] + jnp.arange(tm, dtype=jnp.int32)[None, :]
    valid = rank < counts[tile_expert][:, None]
    src = jnp.clip(raw_start[tile_expert][:, None] + rank, 0, a - 1)
    row_token = jnp.where(valid, order[src] // TOP_K, 0).reshape(ntiles * tm)
    sorted_pos = jnp.argsort(order).astype(jnp.int32)
    shift = jnp.sum(jnp.where(onehot, (pad_start - raw_start)[None, :], 0), axis=1)
    pos = sorted_pos + shift
    n_used = (pad_end[-1] // tm).reshape(1).astype(jnp.int32)
    return row_token, pos.reshape(n, TOP_K), tile_expert, n_used


def _moe(x, gn, router, wg, wu, wd, *, tm=512):
    n = x.shape[0]
    tm = min(tm, n)
    h, idx, gate = _router(x, gn, router)
    row_token, pos, tile_expert, n_used = _route_plan(idx, tm)
    rows = 2 * tm if row_token.shape[0] % (2 * tm) == 0 else tm
    h_sorted = _gather(h, row_token, rows=rows, out_dtype=BF16)
    y_sorted = _moe_ffn(tile_expert, n_used, h_sorted, wg, wu, wd, tm=tm)
    return _combine(y_sorted, pos, gate, x)


def kernel(x, attn_norm, w_in, diff_q_norm, diff_k_norm, diff_lambda, diff_subln, gqa_q_norm, gqa_k_norm, w_out, ffn_norm, dense_w_gate, dense_w_up, dense_w_down, moe_router, moe_w_gate, moe_w_up, moe_w_down):
    batch, seq, d = x.shape
    depth = w_in.shape[0]
    n = batch * seq
    xs = x.reshape(n, d).astype(F32)
    tables = _position_tables(seq, min(ATTN_TILE, seq))
    slopes = tuple(2.0 ** (-8.0 * (h + 1) / DIFF_HEADS) for h in range(DIFF_HEADS))
    ff_tile = 512
    bound_margin = 1.02

    for l in range(depth):
        amax = lambda g: jnp.max(jnp.abs(g.astype(F32)))
        diff_bound = amax(diff_q_norm[l]) * amax(diff_k_norm[l]) * (DIFF_HALF ** 0.5 * bound_margin)
        gqa_bound = amax(gqa_q_norm[l]) * amax(gqa_k_norm[l]) * (HEAD_DIM ** 0.5 * bound_margin)
        diff_ok = diff_bound <= SHIFT_MAX
        diff_shift = jnp.where(diff_ok, diff_bound, 0.0)
        aq1, aq2, ak1, ak2, av, bq, bk, bv = _in_proj(
            xs, attn_norm[l], w_in[l].astype(BF16), tables, diff_shift,
            diff_q_norm[l], diff_k_norm[l], gqa_q_norm[l], gqa_k_norm[l], seq=seq, slopes=slopes)
        lam_init = 0.8 - 0.6 * math.exp(-0.3 * l)
        lp = diff_lambda[l].astype(F32)
        lam = jnp.exp(jnp.sum(lp[0] * lp[1])) - jnp.exp(jnp.sum(lp[2] * lp[3])) + lam_init
        ao = _diff_attention(aq1, aq2, ak1, ak2, av, lam, diff_subln[l], diff_ok,
                             batch=batch, seq=seq, out_scale=1.0 - lam_init)
        bo = _gqa_attention(bq, bk, bv, gqa_bound, batch=batch, seq=seq)
        xs = _out_proj(ao, bo, w_out[l].astype(BF16), xs)

        i = l // 2
        if l % 2 == 0:
            wg = _pad_ff(dense_w_gate[i].astype(BF16), 1, ff_tile)
            wu = _pad_ff(dense_w_up[i].astype(BF16), 1, ff_tile)
            wd = _pad_ff(dense_w_down[i].astype(BF16), 0, ff_tile)
            xs = _ffn_dense(xs, ffn_norm[l], wg, wu, wd, tf=ff_tile)
        else:
            xs = _moe(xs, ffn_norm[l], moe_router[i].astype(BF16), moe_w_gate[i].astype(BF16),
                      moe_w_up[i].astype(BF16), moe_w_down[i].astype(BF16))
    return xs.reshape(batch, seq, d)
```
